```python
import math
import jax, jax.numpy as jnp
from jax import lax
import numpy as np

D_MODEL = 1024
BATCH = 1
SEQ = 16384
DEPTH = 1

HEAD_DIM = 64
HEADS_PER_GROUP = 8
DILATED_GROUPS = ((128, 1), (512, 4), (2048, 16))
N_GROUPS = len(DILATED_GROUPS)
N_ATTN_HEADS = N_GROUPS * HEADS_PER_GROUP
ATTN_W = N_ATTN_HEADS * HEAD_DIM
ATTN_OUT_W = HEADS_PER_GROUP * HEAD_DIM
BLOCK = 128
REL_BUCKETS = 32
REL_MAX_DISTANCE = 2048
CONV_CHANNELS = D_MODEL
CONV_WIDTH = 31
FFN_HIDDEN = -(-8 * D_MODEL // (3 * 256)) * 256
IN_W = 3 * ATTN_W + 2 * CONV_CHANNELS + 2 * D_MODEL
RMS_EPS = 1e-6
LN_EPS = 1e-5
NEG_INF = -1e30

kernel_name = "hybrid_dilated_attn_conformer_conv_gated_block"


def rms_norm(x, g):
    xf = x.astype(jnp.float32)
    y = xf * lax.rsqrt(jnp.mean(xf * xf, axis=-1, keepdims=True) + RMS_EPS)
    return (y * g.astype(jnp.float32)).astype(x.dtype)


def layer_norm(x, g, b):
    xf = x.astype(jnp.float32)
    mu = jnp.mean(xf, axis=-1, keepdims=True)
    xc = xf - mu
    y = xc * lax.rsqrt(jnp.mean(xc * xc, axis=-1, keepdims=True) + LN_EPS)
    return (y * g.astype(jnp.float32) + b.astype(jnp.float32)).astype(x.dtype)


def rel_bucket(dist):
    max_exact = REL_BUCKETS // 2
    d = jnp.maximum(dist, 0)
    df = jnp.maximum(d, 1).astype(jnp.float32)
    large = max_exact + (jnp.log(df / max_exact) / math.log(REL_MAX_DISTANCE / max_exact)
                         * (REL_BUCKETS - max_exact)).astype(jnp.int32)
    large = jnp.minimum(large, REL_BUCKETS - 1)
    return jnp.where(d < max_exact, d, large)


def dilated_group_attention(q, k, v, bias_tab, window, dilation):
    B, S, H, Dh = q.shape
    span = window // dilation
    L = S // dilation
    nb = -(-L // BLOCK)
    Lp = nb * BLOCK
    n_prev = -(-span // BLOCK)
    kb_len = (n_prev + 1) * BLOCK

    def to_sub(t, front):
        t = t.reshape(B, L, dilation, H, Dh).transpose(0, 2, 1, 3, 4)
        return jnp.pad(t, ((0, 0), (0, 0), (front, Lp - L), (0, 0), (0, 0)))

    qb = to_sub(q, 0).reshape(B, dilation, nb, BLOCK, H, Dh)

    def band(t):
        tb = to_sub(t, n_prev * BLOCK).reshape(B, dilation, nb + n_prev, BLOCK, H, Dh)
        return jnp.concatenate([tb[:, :, j:j + nb] for j in range(n_prev + 1)], axis=3)

    kb, vb = band(k), band(v)
    a = jnp.arange(BLOCK, dtype=jnp.int32)[:, None]
    c = jnp.arange(kb_len, dtype=jnp.int32)[None, :]
    offset = a - c + n_prev * BLOCK
    bias = bias_tab[rel_bucket(offset * dilation)].astype(jnp.float32).transpose(2, 0, 1)
    kj = (jnp.arange(nb, dtype=jnp.int32)[:, None, None] - n_prev) * BLOCK + c[None]
    valid = (offset >= 0) & (offset <= span) & (kj >= 0)

    s = jnp.einsum('brnqhd,brnkhd->brnhqk', qb, kb).astype(jnp.float32) * (Dh ** -0.5) + bias
    s = jnp.where(valid[:, None], s, NEG_INF)
    lse = jax.nn.logsumexp(s, axis=-1)
    p = jnp.exp(s - lse[..., None])
    o = jnp.einsum('brnhqk,brnkhd->brnqhd', p, vb.astype(jnp.float32))
    o = o.reshape(B, dilation, Lp, H, Dh)[:, :, :L].transpose(0, 2, 1, 3, 4).reshape(B, S, H, Dh)
    lse = lse.transpose(0, 1, 2, 4, 3).reshape(B, dilation, Lp, H)[:, :, :L]
    lse = lse.transpose(0, 2, 1, 3).reshape(B, S, H)
    return o, lse


def dilated_attention_mixer(q, k, v, rel_bias_table, w_attn_out):
    B, S = q.shape[0], q.shape[1]
    outs, lses = [], []
    for g, (window, dilation) in enumerate(DILATED_GROUPS):
        tab = rel_bias_table[:, g * HEADS_PER_GROUP:(g + 1) * HEADS_PER_GROUP]
        o_g, lse_g = dilated_group_attention(q[:, :, g], k[:, :, g], v[:, :, g], tab, window, dilation)
        outs.append(o_g)
        lses.append(lse_g)
    alpha = jax.nn.softmax(jnp.stack(lses, axis=0), axis=0)
    o = jnp.einsum('gbsh,gbshd->bshd', alpha, jnp.stack(outs, axis=0))
    o = o.reshape(B, S, ATTN_OUT_W).astype(w_attn_out.dtype)
    return o @ w_attn_out


def conformer_conv_mixer(glu_in, b_glu, w_dw, b_dw, g_ln, b_ln, w_conv_out, b_conv_out):
    h = glu_in + b_glu
    u, gate = jnp.split(h, 2, axis=-1)
    u = u * jax.nn.sigmoid(gate)
    u = lax.conv_general_dilated(
        u, w_dw.reshape(CONV_WIDTH, 1, CONV_CHANNELS).astype(u.dtype),
        window_strides=(1,), padding=[(CONV_WIDTH - 1, 0)],
        dimension_numbers=('NWC', 'WIO', 'NWC'),
        feature_group_count=CONV_CHANNELS) + b_dw
    u = jax.nn.silu(layer_norm(u, g_ln, b_ln))
    return u @ w_conv_out + b_conv_out


def swiglu_ffn(h, w_ffn_in, w_ffn_out):
    gate, up = jnp.split(h @ w_ffn_in, 2, axis=-1)
    return (jax.nn.silu(gate) * up) @ w_ffn_out


def setup_inputs(seed: int = 0) -> dict:
    key = jax.random.key(seed)
    ks = jax.random.split(key, 20)
    f32 = jnp.float32

    def nrm(k, shape, scale):
        return jax.random.normal(k, shape, f32) * scale

    def gain(k, shape):
        return 1.0 + 0.05 * jax.random.normal(k, shape, f32)

    return {
        "x": jax.random.normal(ks[0], (BATCH, SEQ, D_MODEL), f32),
        "rel_bias_table": nrm(ks[1], (REL_BUCKETS, N_ATTN_HEADS), 0.2),
        "g_pre_mix": gain(ks[2], (DEPTH, D_MODEL)),
        "w_in": nrm(ks[3], (DEPTH, D_MODEL, IN_W), D_MODEL ** -0.5),
        "b_glu": nrm(ks[4], (DEPTH, 2 * CONV_CHANNELS), 0.02),
        "w_dw": nrm(ks[5], (DEPTH, CONV_WIDTH, CONV_CHANNELS), CONV_WIDTH ** -0.5),
        "b_dw": nrm(ks[6], (DEPTH, CONV_CHANNELS), 0.02),
        "g_conv_ln": gain(ks[7], (DEPTH, CONV_CHANNELS)),
        "b_conv_ln": nrm(ks[8], (DEPTH, CONV_CHANNELS), 0.02),
        "w_conv_out": nrm(ks[9], (DEPTH, CONV_CHANNELS, D_MODEL), CONV_CHANNELS ** -0.5),
        "b_conv_out": nrm(ks[10], (DEPTH, D_MODEL), 0.02),
        "w_attn_out": nrm(ks[11], (DEPTH, ATTN_OUT_W, D_MODEL), ATTN_OUT_W ** -0.5),
        "w_mix_out": nrm(ks[12], (DEPTH, D_MODEL, D_MODEL), D_MODEL ** -0.5),
        "g_post_mix": gain(ks[13], (DEPTH, D_MODEL)),
        "g_pre_ffn": gain(ks[14], (DEPTH, D_MODEL)),
        "w_ffn_in": nrm(ks[15], (DEPTH, D_MODEL, 2 * FFN_HIDDEN), D_MODEL ** -0.5),
        "w_ffn_out": nrm(ks[16], (DEPTH, FFN_HIDDEN, D_MODEL), FFN_HIDDEN ** -0.5),
        "g_post_ffn": gain(ks[17], (DEPTH, D_MODEL)),
    }


def reference(x, rel_bias_table, g_pre_mix, w_in, b_glu, w_dw, b_dw, g_conv_ln, b_conv_ln,
              w_conv_out, b_conv_out, w_attn_out, w_mix_out, g_post_mix, g_pre_ffn,
              w_ffn_in, w_ffn_out, g_post_ffn):
    B, S, D = x.shape
    for l in range(DEPTH):
        h = rms_norm(x, g_pre_mix[l])
        z = h @ w_in[l]
        q, k, v, glu_in, z_ga, z_gc = jnp.split(
            z, np.cumsum([ATTN_W, ATTN_W, ATTN_W, 2 * CONV_CHANNELS, D_MODEL]).tolist(), axis=-1)
        shp = (B, S, N_GROUPS, HEADS_PER_GROUP, HEAD_DIM)
        y_attn = dilated_attention_mixer(q.reshape(shp), k.reshape(shp), v.reshape(shp),
                                         rel_bias_table, w_attn_out[l])
        y_conv = conformer_conv_mixer(glu_in, b_glu[l], w_dw[l], b_dw[l], g_conv_ln[l],
                                      b_conv_ln[l], w_conv_out[l], b_conv_out[l])
        merged = jax.nn.sigmoid(z_ga) * y_attn + jax.nn.sigmoid(z_gc) * y_conv
        x = x + rms_norm(merged @ w_mix_out[l], g_post_mix[l])
        h = rms_norm(x, g_pre_ffn[l])
        x = x + rms_norm(swiglu_ffn(h, w_ffn_in[l], w_ffn_out[l]), g_post_ffn[l])
    return x
```

```python
import functools
import math

import jax
import jax.numpy as jnp
import numpy as np
from jax import lax
from jax.experimental import pallas as pl
from jax.experimental.pallas import tpu as pltpu

D_MODEL = 1024
HEAD_DIM = 64
HEADS_PER_GROUP = 8
DILATED_GROUPS = ((128, 1), (512, 4), (2048, 16))
N_GROUPS = len(DILATED_GROUPS)
GROUP_W = HEADS_PER_GROUP * HEAD_DIM
ATTN_W = N_GROUPS * GROUP_W
BLOCK = 128
REL_BUCKETS = 32
REL_MAX_DISTANCE = 2048
CONV_CHANNELS = D_MODEL
CONV_WIDTH = 31
FFN_HIDDEN = 2816
IN_W = 3 * ATTN_W + 2 * CONV_CHANNELS + 2 * D_MODEL
RMS_EPS = 1e-6
LN_EPS = 1e-5
NEG_INF = -1e30

COLS_PER_TOKEN = IN_W // GROUP_W
Q_COL, K_COL, V_COL = 0, 3, 6
GLU_U_COL, GLU_G_COL, GATE_A_COL, GATE_C_COL = 9, 11, 13, 15

LSE_W = 128
HALO = 32
VMEM_LIMIT = 56 * 1024 * 1024

BF16 = jnp.bfloat16
F32 = jnp.float32


def _sigmoid(v):
    return 1.0 / (1.0 + jnp.exp(-v))


def _rms(v, g):
    return v * lax.rsqrt(jnp.mean(v * v, axis=-1, keepdims=True) + RMS_EPS) * g


def _inproj_kernel(x_ref, g_ref, w_ref, o_ref, h_ref):
    @pl.when(pl.program_id(1) == 0)
    def _():
        h_ref[...] = _rms(x_ref[...], g_ref[...]).astype(BF16)

    o_ref[...] = jnp.dot(h_ref[...], w_ref[...], preferred_element_type=F32).astype(BF16)


def _in_proj(x2, g, w_bf16, tm=1024, tn=2176):
    S = x2.shape[0]
    return pl.pallas_call(
        _inproj_kernel,
        grid=(S // tm, IN_W // tn),
        in_specs=[
            pl.BlockSpec((tm, D_MODEL), lambda i, j: (i, 0)),
            pl.BlockSpec((1, D_MODEL), lambda i, j: (0, 0)),
            pl.BlockSpec((D_MODEL, tn), lambda i, j: (0, j)),
        ],
        out_specs=pl.BlockSpec((tm, tn), lambda i, j: (i, j)),
        out_shape=jax.ShapeDtypeStruct((S, IN_W), BF16),
        scratch_shapes=[pltpu.VMEM((tm, D_MODEL), BF16)],
        compiler_params=pltpu.CompilerParams(
            dimension_semantics=("parallel", "arbitrary"), vmem_limit_bytes=VMEM_LIMIT),
        name="in_proj",
    )(x2, g, w_bf16)


def _attn_kernel(q_ref, kp_ref, kc_ref, vp_ref, vc_ref, bias_ref, o_ref, lse_ref):
    first = pl.program_id(1) == 0
    lane = lax.broadcasted_iota(jnp.int32, (BLOCK, LSE_W), 1)
    col = lax.broadcasted_iota(jnp.int32, (BLOCK, 2 * BLOCK), 1)
    prev_pad = jnp.where(first & (col < BLOCK), NEG_INF, 0.0).astype(F32)
    lse_tile = jnp.zeros((BLOCK, LSE_W), F32)
    for h in range(HEADS_PER_GROUP):
        sl = slice(h * HEAD_DIM, (h + 1) * HEAD_DIM)
        q = q_ref[:, sl] * jnp.asarray(HEAD_DIM ** -0.5, BF16)
        k = jnp.concatenate([kp_ref[:, sl], kc_ref[:, sl]], axis=0)
        v = jnp.concatenate([vp_ref[:, sl], vc_ref[:, sl]], axis=0)
        s = lax.dot_general(q, k, (((1,), (1,)), ((), ())), preferred_element_type=F32)
        s = s + bias_ref[h] + prev_pad
        m = jnp.max(s, axis=-1, keepdims=True)
        p = jnp.exp(s - m)
        l = jnp.sum(p, axis=-1, keepdims=True)
        o = jnp.dot(p.astype(BF16), v, preferred_element_type=F32) / l
        o_ref[:, sl] = o.astype(o_ref.dtype)
        lse_tile = jnp.where(lane == h, m + jnp.log(l), lse_tile)
    lse_ref[...] = lse_tile


def _attention_group(z, bias, g, dilation):
    S = z.shape[0]
    L = S // dilation
    nb = L // BLOCK
    zv = z.reshape(L, dilation * IN_W)

    def spec(col0, prev):
        if prev:
            return pl.BlockSpec((BLOCK, GROUP_W),
                                lambda c, i: (jnp.maximum(i - 1, 0), c * COLS_PER_TOKEN + col0 + g))
        return pl.BlockSpec((BLOCK, GROUP_W), lambda c, i: (i, c * COLS_PER_TOKEN + col0 + g))

    o, lse = pl.pallas_call(
        _attn_kernel,
        grid=(dilation, nb),
        in_specs=[
            spec(Q_COL, False),
            spec(K_COL, True), spec(K_COL, False),
            spec(V_COL, True), spec(V_COL, False),
            pl.BlockSpec((HEADS_PER_GROUP, BLOCK, 2 * BLOCK), lambda c, i: (0, 0, 0)),
        ],
        out_specs=[
            pl.BlockSpec((BLOCK, GROUP_W), lambda c, i: (i, c)),
            pl.BlockSpec((BLOCK, LSE_W), lambda c, i: (i, c)),
        ],
        out_shape=[
            jax.ShapeDtypeStruct((L, dilation * GROUP_W), BF16),
            jax.ShapeDtypeStruct((L, dilation * LSE_W), F32),
        ],
        compiler_params=pltpu.CompilerParams(
            dimension_semantics=("parallel", "arbitrary"), vmem_limit_bytes=VMEM_LIMIT),
        name=f"attn_g{g}",
    )(zv, zv, zv, zv, zv, bias)
    return o.reshape(S, GROUP_W), lse.reshape(S, LSE_W)


def _rel_bucket(dist):
    max_exact = REL_BUCKETS // 2
    d = jnp.maximum(dist, 0)
    df = jnp.maximum(d, 1).astype(F32)
    large = max_exact + (jnp.log(df / max_exact) / math.log(REL_MAX_DISTANCE / max_exact)
                         * (REL_BUCKETS - max_exact)).astype(jnp.int32)
    large = jnp.minimum(large, REL_BUCKETS - 1)
    return jnp.where(d < max_exact, d, large)


def _bias_table(rel_bias_table, g, window, dilation):
    span = window // dilation
    assert span == BLOCK
    a = jnp.arange(BLOCK, dtype=jnp.int32)[:, None]
    c = jnp.arange(2 * BLOCK, dtype=jnp.int32)[None, :]
    offset = a - c + BLOCK
    tab = rel_bias_table[:, g * HEADS_PER_GROUP:(g + 1) * HEADS_PER_GROUP].astype(F32)
    bias = tab[_rel_bucket(offset * dilation)]
    valid = (offset >= 0) & (offset <= span)
    return jnp.where(valid[..., None], bias, NEG_INF).transpose(2, 0, 1)


def _mixer_kernel(o0_ref, o1_ref, o2_ref, l0_ref, l1_ref, l2_ref,
                  u0_ref, u1_ref, g0_ref, g1_ref, hu0_ref, hu1_ref, hg0_ref, hg1_ref,
                  ga0_ref, ga1_ref, gc0_ref, gc1_ref, x_ref,
                  expand_ref, b_glu_ref, w_dw_ref, b_dw_ref, g_ln_ref, b_ln_ref,
                  w_co_ref, b_co_ref, w_ao_ref, w_mo_ref, g_pm_ref,
                  out_ref, ext_ref):
    tm = x_ref.shape[0]
    C = CONV_CHANNELS
    l0, l1, l2 = l0_ref[...], l1_ref[...], l2_ref[...]
    lm = jnp.maximum(jnp.maximum(l0, l1), l2)
    e0, e1, e2 = jnp.exp(l0 - lm), jnp.exp(l1 - lm), jnp.exp(l2 - lm)
    inv = 1.0 / (e0 + e1 + e2)
    expand = expand_ref[...]

    def widen(a):
        hi = a.astype(BF16)
        lo = (a - hi.astype(F32)).astype(BF16)
        return (jnp.dot(hi, expand, preferred_element_type=F32)
                + jnp.dot(lo, expand, preferred_element_type=F32))

    o = (widen(e0 * inv) * o0_ref[...].astype(F32)
         + widen(e1 * inv) * o1_ref[...].astype(F32)
         + widen(e2 * inv) * o2_ref[...].astype(F32))
    y_attn = jnp.dot(o.astype(BF16), w_ao_ref[...], preferred_element_type=F32)

    b_u = b_glu_ref[:, :C]
    b_g = b_glu_ref[:, C:]

    def glu(a0, a1, g0, g1):
        a = jnp.concatenate([a0[...], a1[...]], axis=-1).astype(F32) + b_u
        gt = jnp.concatenate([g0[...], g1[...]], axis=-1).astype(F32) + b_g
        return a * _sigmoid(gt)

    halo = glu(hu0_ref, hu1_ref, hg0_ref, hg1_ref)
    halo = jnp.where(pl.program_id(0) == 0, 0.0, halo)
    ext_ref[0:HALO, :] = halo
    ext_ref[HALO:HALO + tm, :] = glu(u0_ref, u1_ref, g0_ref, g1_ref)
    acc = jnp.zeros((tm, C), F32) + b_dw_ref[...]
    for j in range(CONV_WIDTH):
        off = HALO - (CONV_WIDTH - 1) + j
        acc = acc + ext_ref[off:off + tm, :] * w_dw_ref[j:j + 1, :]
    mu = jnp.mean(acc, axis=-1, keepdims=True)
    xc = acc - mu
    y = xc * lax.rsqrt(jnp.mean(xc * xc, axis=-1, keepdims=True) + LN_EPS)
    y = y * g_ln_ref[...] + b_ln_ref[...]
    y = y * _sigmoid(y)
    y_conv = jnp.dot(y.astype(BF16), w_co_ref[...], preferred_element_type=F32) + b_co_ref[...]

    ga = jnp.concatenate([ga0_ref[...], ga1_ref[...]], axis=-1).astype(F32)
    gc = jnp.concatenate([gc0_ref[...], gc1_ref[...]], axis=-1).astype(F32)
    merged = _sigmoid(ga) * y_attn + _sigmoid(gc) * y_conv
    mix = jnp.dot(merged.astype(BF16), w_mo_ref[...], preferred_element_type=F32)
    out_ref[...] = x_ref[...] + _rms(mix, g_pm_ref[...])


def _mixer(x2, z, os_, lses, params, tm=512):
    S = x2.shape[0]
    hb = tm // HALO

    def zspec(col):
        return pl.BlockSpec((tm, GROUP_W), lambda i: (i, col))

    def hspec(col):
        return pl.BlockSpec((HALO, GROUP_W), lambda i: (jnp.maximum(i * hb - 1, 0), col))

    def full(a):
        return pl.BlockSpec(a.shape, lambda i: (0,) * a.ndim)

    row = lambda w: pl.BlockSpec((tm, w), lambda i: (i, 0))
    z_specs = ([zspec(GLU_U_COL), zspec(GLU_U_COL + 1), zspec(GLU_G_COL), zspec(GLU_G_COL + 1)]
               + [hspec(GLU_U_COL), hspec(GLU_U_COL + 1), hspec(GLU_G_COL), hspec(GLU_G_COL + 1)]
               + [zspec(GATE_A_COL), zspec(GATE_A_COL + 1), zspec(GATE_C_COL), zspec(GATE_C_COL + 1)])
    return pl.pallas_call(
        _mixer_kernel,
        grid=(S // tm,),
        in_specs=[row(GROUP_W)] * 3 + [row(LSE_W)] * 3 + z_specs + [row(D_MODEL)]
                 + [full(p) for p in params],
        out_specs=row(D_MODEL),
        out_shape=jax.ShapeDtypeStruct((S, D_MODEL), F32),
        scratch_shapes=[pltpu.VMEM((HALO + tm, CONV_CHANNELS), F32)],
        compiler_params=pltpu.CompilerParams(
            dimension_semantics=("parallel",), vmem_limit_bytes=VMEM_LIMIT),
        name="mixer",
    )(*os_, *lses, *([z] * 12), x2, *params)


def _ffn_kernel(x_ref, g_pre_ref, w_in_ref, w_out_ref, g_post_ref, out_ref, *, chunk):
    x = x_ref[...]
    h = _rms(x, g_pre_ref[...]).astype(BF16)
    acc = jnp.zeros(x.shape, F32)
    for c0 in range(0, FFN_HIDDEN, chunk):
        gate = jnp.dot(h, w_in_ref[:, c0:c0 + chunk], preferred_element_type=F32)
        up = jnp.dot(h, w_in_ref[:, FFN_HIDDEN + c0:FFN_HIDDEN + c0 + chunk],
                     preferred_element_type=F32)
        act = (gate * _sigmoid(gate) * up).astype(BF16)
        acc = acc + jnp.dot(act, w_out_ref[c0:c0 + chunk, :], preferred_element_type=F32)
    out_ref[...] = x + _rms(acc, g_post_ref[...])


def _ffn(x1, g_pre, w_in, w_out, g_post, tm=512, chunk=1408):
    S = x1.shape[0]
    row = pl.BlockSpec((tm, D_MODEL), lambda i: (i, 0))

    def full(a):
        return pl.BlockSpec(a.shape, lambda i: (0,) * a.ndim, pipeline_mode=pl.Buffered(1))

    return pl.pallas_call(
        functools.partial(_ffn_kernel, chunk=chunk),
        grid=(S // tm,),
        in_specs=[row, full(g_pre), full(w_in), full(w_out), full(g_post)],
        out_specs=row,
        out_shape=jax.ShapeDtypeStruct((S, D_MODEL), F32),
        compiler_params=pltpu.CompilerParams(
            dimension_semantics=("parallel",), vmem_limit_bytes=VMEM_LIMIT),
        name="ffn",
    )(x1, g_pre, w_in, w_out, g_post)


def kernel(x, rel_bias_table, g_pre_mix, w_in, b_glu, w_dw, b_dw, g_conv_ln, b_conv_ln, w_conv_out, b_conv_out, w_attn_out, w_mix_out, g_post_mix, g_pre_ffn, w_ffn_in, w_ffn_out, g_post_ffn):
    B, S, D = x.shape
    depth = w_in.shape[0]
    expand = jnp.asarray(
        (np.arange(LSE_W)[:, None] == np.arange(GROUP_W)[None, :] // HEAD_DIM), dtype=BF16)
    outs = []
    for b in range(B):
        xb = x[b]
        for l in range(depth):
            r2 = lambda a: a[l].reshape(1, -1)
            z = _in_proj(xb, r2(g_pre_mix), w_in[l].astype(BF16))
            os_, lses = [], []
            for g, (window, dilation) in enumerate(DILATED_GROUPS):
                o_g, lse_g = _attention_group(z, _bias_table(rel_bias_table, g, window, dilation), g, dilation)
                os_.append(o_g)
                lses.append(lse_g)
            params = (expand, r2(b_glu), w_dw[l], r2(b_dw), r2(g_conv_ln), r2(b_conv_ln),
                      w_conv_out[l].astype(BF16), r2(b_conv_out), w_attn_out[l].astype(BF16),
                      w_mix_out[l].astype(BF16), r2(g_post_mix))
            x1 = _mixer(xb, z, os_, lses, params)
            xb = _ffn(x1, r2(g_pre_ffn), w_ffn_in[l].astype(BF16), w_ffn_out[l].astype(BF16),
                      r2(g_post_ffn))
        outs.append(xb)
    return jnp.stack(outs, axis=0)
```

```python
import functools
import math

import jax
import jax.numpy as jnp
import numpy as np
from jax import lax
from jax.experimental import pallas as pl
from jax.experimental.pallas import tpu as pltpu

D_MODEL = 1024
HEAD_DIM = 64
HEADS_PER_GROUP = 8
DILATED_GROUPS = ((128, 1), (512, 4), (2048, 16))
N_GROUPS = len(DILATED_GROUPS)
GROUP_W = HEADS_PER_GROUP * HEAD_DIM
ATTN_W = N_GROUPS * GROUP_W
QKV_W = 3 * GROUP_W
BLOCK = 128
REL_BUCKETS = 32
REL_MAX_DISTANCE = 2048
CONV_CHANNELS = D_MODEL
CONV_WIDTH = 31
FFN_HIDDEN = 2816
RMS_EPS = 1e-6
LN_EPS = 1e-5
NEG_INF = -1e30

NAT_W = 2 * CONV_CHANNELS + 2 * D_MODEL + QKV_W
NAT_QKV_COL = (2 * CONV_CHANNELS + 2 * D_MODEL) // GROUP_W

LANES = 128
LSE_W = LANES
HALO = 32
VMEM_LIMIT = 56 * 1024 * 1024

BF16 = jnp.bfloat16
F32 = jnp.float32


def _sigmoid(v):
    return 1.0 / (1.0 + jnp.exp(-v))


def _rms(v, g):
    return v * lax.rsqrt(jnp.mean(v * v, axis=-1, keepdims=True) + RMS_EPS) * g


def _inproj_kernel(*refs, r):
    x_refs, (g_ref, w_ref, o_ref, h_ref) = refs[:-4], refs[-4:]
    tm = x_refs[0].shape[0]
    n = tm // r

    @pl.when(pl.program_id(1) == 0)
    def _():
        if r == 1:
            h_ref[...] = _rms(x_refs[0][...], g_ref[...]).astype(BF16)
        else:
            for c in range(r):
                xs = jnp.concatenate([xr[pl.ds(c, n, stride=r), :] for xr in x_refs], axis=-1)
                h_ref[c * n:(c + 1) * n, :] = _rms(xs, g_ref[...]).astype(BF16)

    z = jnp.dot(h_ref[...], w_ref[...], preferred_element_type=F32).astype(BF16)
    o_ref[...] = z.reshape(o_ref.shape)


def _in_proj(x2, g, w_bf16, r, tm, tn):
    S = x2.shape[0]
    N = w_bf16.shape[1]
    if r == 1:
        x_specs = [pl.BlockSpec((tm, D_MODEL), lambda i, j: (i, 0))]
    else:
        x_specs = [pl.BlockSpec((tm, LANES), lambda i, j, k=k: (i, k)) for k in range(D_MODEL // LANES)]
    return pl.pallas_call(
        functools.partial(_inproj_kernel, r=r),
        grid=(S // tm, N // tn),
        in_specs=x_specs + [
            pl.BlockSpec((1, D_MODEL), lambda i, j: (0, 0)),
            pl.BlockSpec((D_MODEL, tn), lambda i, j: (0, j)),
        ],
        out_specs=pl.BlockSpec((r, tm // r, tn), lambda i, j: (0, i, j)),
        out_shape=jax.ShapeDtypeStruct((r, S // r, N), BF16),
        scratch_shapes=[pltpu.VMEM((tm, D_MODEL), BF16)],
        compiler_params=pltpu.CompilerParams(
            dimension_semantics=("parallel", "arbitrary"), vmem_limit_bytes=VMEM_LIMIT),
        name=f"in_proj_r{r}",
    )(*([x2] * len(x_specs)), g, w_bf16)


def _bucket_maps():
    max_exact = REL_BUCKETS // 2
    a = np.arange(BLOCK, dtype=np.int64)[:, None]
    c = np.arange(2 * BLOCK, dtype=np.int64)[None, :]
    offset = a - c + BLOCK
    maps = []
    for window, dilation in DILATED_GROUPS:
        span = window // dilation
        assert span == BLOCK
        d = np.maximum(offset * dilation, 0)
        df = np.maximum(d, 1).astype(np.float32)
        large = max_exact + (np.log(df / np.float32(max_exact)) / np.float32(math.log(REL_MAX_DISTANCE / max_exact))
                             * np.float32(REL_BUCKETS - max_exact)).astype(np.int32)
        large = np.minimum(large, REL_BUCKETS - 1)
        bucket = np.where(d < max_exact, d, large)
        valid = (offset >= 0) & (offset <= span)
        maps.append(np.where(valid, bucket, -1).astype(np.int32))
    return np.stack(maps)


def _bias_kernel(tab_ref, bucket_ref, o_ref):
    g = pl.program_id(0)
    bucket = bucket_ref[...]
    for h in range(HEADS_PER_GROUP):
        acc = jnp.full(bucket.shape, NEG_INF, F32)
        for b in range(REL_BUCKETS):
            acc = jnp.where(bucket == b, tab_ref[b, g * HEADS_PER_GROUP + h], acc)
        o_ref[h] = acc


def _bias_tables(rel_bias_table):
    return pl.pallas_call(
        _bias_kernel,
        grid=(N_GROUPS,),
        in_specs=[
            pl.BlockSpec(memory_space=pltpu.SMEM),
            pl.BlockSpec((None, BLOCK, 2 * BLOCK), lambda g: (g, 0, 0)),
        ],
        out_specs=pl.BlockSpec((None, HEADS_PER_GROUP, BLOCK, 2 * BLOCK), lambda g: (g, 0, 0, 0)),
        out_shape=jax.ShapeDtypeStruct((N_GROUPS, HEADS_PER_GROUP, BLOCK, 2 * BLOCK), F32),
        name="bias_tables",
    )(rel_bias_table.astype(F32), jnp.asarray(_bucket_maps()))


def _attn_kernel(q_ref, kp_ref, kc_ref, vp_ref, vc_ref, bias_ref, o_ref, lse_ref):
    first = pl.program_id(1) == 0
    lane = lax.broadcasted_iota(jnp.int32, (BLOCK, LSE_W), 1)
    col = lax.broadcasted_iota(jnp.int32, (BLOCK, 2 * BLOCK), 1)
    prev_pad = jnp.where(first & (col < BLOCK), NEG_INF, 0.0).astype(F32)
    lse_tile = jnp.zeros((BLOCK, LSE_W), F32)
    for h in range(HEADS_PER_GROUP):
        sl = slice(h * HEAD_DIM, (h + 1) * HEAD_DIM)
        q = q_ref[:, sl] * jnp.asarray(HEAD_DIM ** -0.5, BF16)
        k = jnp.concatenate([kp_ref[:, sl], kc_ref[:, sl]], axis=0)
        v = jnp.concatenate([vp_ref[:, sl], vc_ref[:, sl]], axis=0)
        s = lax.dot_general(q, k, (((1,), (1,)), ((), ())), preferred_element_type=F32)
        s = s + bias_ref[h] + prev_pad
        m = jnp.max(s, axis=-1, keepdims=True)
        p = jnp.exp(s - m)
        l = jnp.sum(p, axis=-1, keepdims=True)
        o = jnp.dot(p.astype(BF16), v, preferred_element_type=F32) / l
        o_ref[:, sl] = o.astype(o_ref.dtype)
        lse_tile = jnp.where(lane == h, m + jnp.log(l), lse_tile)
    lse_ref[...] = lse_tile


def _attention_group(qkv, qkv_col, biases, g):
    r, L, _ = qkv.shape
    nb = L // BLOCK

    def spec(part, prev):
        if prev:
            return pl.BlockSpec((None, BLOCK, GROUP_W),
                                lambda c, i: (c, jnp.maximum(i - 1, 0), qkv_col + part))
        return pl.BlockSpec((None, BLOCK, GROUP_W), lambda c, i: (c, i, qkv_col + part))

    return pl.pallas_call(
        _attn_kernel,
        grid=(r, nb),
        in_specs=[
            spec(0, False),
            spec(1, True), spec(1, False),
            spec(2, True), spec(2, False),
            pl.BlockSpec((None, HEADS_PER_GROUP, BLOCK, 2 * BLOCK), lambda c, i: (g, 0, 0, 0)),
        ],
        out_specs=[
            pl.BlockSpec((None, BLOCK, GROUP_W), lambda c, i: (c, i, 0)),
            pl.BlockSpec((None, BLOCK, LSE_W), lambda c, i: (c, i, 0)),
        ],
        out_shape=[
            jax.ShapeDtypeStruct((r, L, GROUP_W), BF16),
            jax.ShapeDtypeStruct((r, L, LSE_W), F32),
        ],
        compiler_params=pltpu.CompilerParams(
            dimension_semantics=("parallel", "arbitrary"), vmem_limit_bytes=VMEM_LIMIT),
        name=f"attn_g{g}",
    )(qkv, qkv, qkv, qkv, qkv, biases)


def _mixer_kernel(o0_ref, o1_ref, o2_ref, l0_ref, l1_ref, l2_ref,
                  u_ref, g_ref, hu_ref, hg_ref, ga_ref, gc_ref, x_ref,
                  expand_ref, b_glu_ref, w_dw_ref, b_dw_ref, g_ln_ref, b_ln_ref,
                  w_co_ref, b_co_ref, w_ao_ref, w_mo_ref, g_pm_ref,
                  out_ref, ext_ref, on_ref, ln_ref):
    tm = x_ref.shape[0]
    C = CONV_CHANNELS

    def natural(src_ref, dst_ref):
        r, n, w = src_ref.shape
        if r == 1:
            return src_ref[0].astype(F32)
        for c in range(r):
            for k in range(w // LANES):
                dst_ref[k, pl.ds(c, n, stride=r), :] = src_ref[c, :, k * LANES:(k + 1) * LANES].astype(F32)
        return jnp.concatenate([dst_ref[k] for k in range(w // LANES)], axis=-1)

    l0 = natural(l0_ref, ln_ref.at[0])
    l1 = natural(l1_ref, ln_ref.at[1])
    l2 = natural(l2_ref, ln_ref.at[2])
    lm = jnp.maximum(jnp.maximum(l0, l1), l2)
    e0, e1, e2 = jnp.exp(l0 - lm), jnp.exp(l1 - lm), jnp.exp(l2 - lm)
    inv = 1.0 / (e0 + e1 + e2)
    expand = expand_ref[...]

    def widen(a):
        hi = a.astype(BF16)
        lo = (a - hi.astype(F32)).astype(BF16)
        return (jnp.dot(hi, expand, preferred_element_type=F32)
                + jnp.dot(lo, expand, preferred_element_type=F32))

    o = (widen(e0 * inv) * natural(o0_ref, on_ref.at[0])
         + widen(e1 * inv) * natural(o1_ref, on_ref.at[1])
         + widen(e2 * inv) * natural(o2_ref, on_ref.at[2]))
    y_attn = jnp.dot(o.astype(BF16), w_ao_ref[...], preferred_element_type=F32)

    b_u = b_glu_ref[:, :C]
    b_g = b_glu_ref[:, C:]

    def glu(a_ref, gt_ref):
        return (a_ref[...].astype(F32) + b_u) * _sigmoid(gt_ref[...].astype(F32) + b_g)

    halo = jnp.where(pl.program_id(0) == 0, 0.0, glu(hu_ref, hg_ref))
    ext_ref[0:HALO, :] = halo
    ext_ref[HALO:HALO + tm, :] = glu(u_ref, g_ref)
    acc = jnp.zeros((tm, C), F32) + b_dw_ref[...]
    for j in range(CONV_WIDTH):
        off = HALO - (CONV_WIDTH - 1) + j
        acc = acc + ext_ref[off:off + tm, :] * w_dw_ref[j:j + 1, :]
    mu = jnp.mean(acc, axis=-1, keepdims=True)
    xc = acc - mu
    y = xc * lax.rsqrt(jnp.mean(xc * xc, axis=-1, keepdims=True) + LN_EPS)
    y = y * g_ln_ref[...] + b_ln_ref[...]
    y = y * _sigmoid(y)
    y_conv = jnp.dot(y.astype(BF16), w_co_ref[...], preferred_element_type=F32) + b_co_ref[...]

    merged = (_sigmoid(ga_ref[...].astype(F32)) * y_attn
              + _sigmoid(gc_ref[...].astype(F32)) * y_conv)
    mix = jnp.dot(merged.astype(BF16), w_mo_ref[...], preferred_element_type=F32)
    out_ref[...] = x_ref[...] + _rms(mix, g_pm_ref[...])


def _mixer(x2, z_nat, os_, lses, params, tm=512):
    S = x2.shape[0]
    hb = tm // HALO
    C = CONV_CHANNELS

    def zspec(col):
        return pl.BlockSpec((tm, C), lambda i: (i, col))

    def hspec(col):
        return pl.BlockSpec((HALO, C), lambda i: (jnp.maximum(i * hb - 1, 0), col))

    def full(a):
        return pl.BlockSpec(a.shape, lambda i: (0,) * a.ndim)

    def rspec(a):
        r, _, w = a.shape
        return pl.BlockSpec((r, tm // r, w), lambda i: (0, i, 0))

    return pl.pallas_call(
        _mixer_kernel,
        grid=(S // tm,),
        in_specs=[rspec(a) for a in os_] + [rspec(a) for a in lses]
                 + [zspec(0), zspec(1), hspec(0), hspec(1), zspec(2), zspec(3)]
                 + [pl.BlockSpec((tm, D_MODEL), lambda i: (i, 0))]
                 + [full(p) for p in params],
        out_specs=pl.BlockSpec((tm, D_MODEL), lambda i: (i, 0)),
        out_shape=jax.ShapeDtypeStruct((S, D_MODEL), F32),
        scratch_shapes=[pltpu.VMEM((HALO + tm, C), F32),
                        pltpu.VMEM((N_GROUPS, GROUP_W // LANES, tm, LANES), F32),
                        pltpu.VMEM((N_GROUPS, LSE_W // LANES, tm, LANES), F32)],
        compiler_params=pltpu.CompilerParams(
            dimension_semantics=("parallel",), vmem_limit_bytes=VMEM_LIMIT),
        name="mixer",
    )(*os_, *lses, *([z_nat] * 6), x2, *params)


def _ffn_kernel(x_ref, g_pre_ref, w_in_ref, w_out_ref, g_post_ref, out_ref, *, chunk):
    x = x_ref[...]
    h = _rms(x, g_pre_ref[...]).astype(BF16)
    acc = jnp.zeros(x.shape, F32)
    for c0 in range(0, FFN_HIDDEN, chunk):
        gate = jnp.dot(h, w_in_ref[:, c0:c0 + chunk], preferred_element_type=F32)
        up = jnp.dot(h, w_in_ref[:, FFN_HIDDEN + c0:FFN_HIDDEN + c0 + chunk],
                     preferred_element_type=F32)
        act = (gate * _sigmoid(gate) * up).astype(BF16)
        acc = acc + jnp.dot(act, w_out_ref[c0:c0 + chunk, :], preferred_element_type=F32)
    out_ref[...] = x + _rms(acc, g_post_ref[...])


def _ffn(x1, g_pre, w_in, w_out, g_post, tm=512, chunk=1408):
    S = x1.shape[0]
    row = pl.BlockSpec((tm, D_MODEL), lambda i: (i, 0))

    def full(a):
        return pl.BlockSpec(a.shape, lambda i: (0,) * a.ndim, pipeline_mode=pl.Buffered(1))

    return pl.pallas_call(
        functools.partial(_ffn_kernel, chunk=chunk),
        grid=(S // tm,),
        in_specs=[row, full(g_pre), full(w_in), full(w_out), full(g_post)],
        out_specs=row,
        out_shape=jax.ShapeDtypeStruct((S, D_MODEL), F32),
        compiler_params=pltpu.CompilerParams(
            dimension_semantics=("parallel",), vmem_limit_bytes=VMEM_LIMIT),
        name="ffn",
    )(x1, g_pre, w_in, w_out, g_post)


def _split_w_in(w):
    q, k, v, rest = (w[:, :ATTN_W], w[:, ATTN_W:2 * ATTN_W], w[:, 2 * ATTN_W:3 * ATTN_W], w[:, 3 * ATTN_W:])

    def qkv(g):
        sl = slice(g * GROUP_W, (g + 1) * GROUP_W)
        return [q[:, sl], k[:, sl], v[:, sl]]

    w_nat = jnp.concatenate([rest] + qkv(0), axis=1).astype(BF16)
    return w_nat, [jnp.concatenate(qkv(g), axis=1).astype(BF16) for g in range(1, N_GROUPS)]


def kernel(x, rel_bias_table, g_pre_mix, w_in, b_glu, w_dw, b_dw, g_conv_ln, b_conv_ln, w_conv_out, b_conv_out, w_attn_out, w_mix_out, g_post_mix, g_pre_ffn, w_ffn_in, w_ffn_out, g_post_ffn):
    B, S, D = x.shape
    depth = w_in.shape[0]
    expand = jnp.asarray(
        (np.arange(LSE_W)[:, None] == np.arange(GROUP_W)[None, :] // HEAD_DIM), dtype=BF16)
    biases = _bias_tables(rel_bias_table)
    outs = []
    for b in range(B):
        xb = x[b]
        for l in range(depth):
            r2 = lambda a: a[l].reshape(1, -1)
            w_nat, w_dil = _split_w_in(w_in[l])
            g_pre = r2(g_pre_mix)
            z_nat = _in_proj(xb, g_pre, w_nat, 1, tm=1024, tn=1408)
            os_, lses = [], []
            for g, (window, dilation) in enumerate(DILATED_GROUPS):
                if g == 0:
                    qkv, col = z_nat, NAT_QKV_COL
                else:
                    qkv, col = _in_proj(xb, g_pre, w_dil[g - 1], dilation, tm=1024, tn=QKV_W), 0
                o_g, lse_g = _attention_group(qkv, col, biases, g)
                os_.append(o_g)
                lses.append(lse_g)
            params = (expand, r2(b_glu), w_dw[l], r2(b_dw), r2(g_conv_ln), r2(b_conv_ln),
                      w_conv_out[l].astype(BF16), r2(b_conv_out), w_attn_out[l].astype(BF16),
                      w_mix_out[l].astype(BF16), r2(g_post_mix))
            x1 = _mixer(xb, z_nat[0], os_, lses, params)
            xb = _ffn(x1, r2(g_pre_ffn), w_ffn_in[l].astype(BF16), w_ffn_out[l].astype(BF16),
                      r2(g_post_ffn))
        outs.append(xb)
    return jnp.stack(outs, axis=0)
```

```python
import functools
import math

import jax
import jax.numpy as jnp
import numpy as np
from jax import lax
from jax.experimental import pallas as pl
from jax.experimental.pallas import tpu as pltpu

D_MODEL = 1024
HEAD_DIM = 64
HEADS_PER_GROUP = 8
DILATED_GROUPS = ((128, 1), (512, 4), (2048, 16))
N_GROUPS = len(DILATED_GROUPS)
GROUP_W = HEADS_PER_GROUP * HEAD_DIM
ATTN_W = N_GROUPS * GROUP_W
QKV_W = 3 * GROUP_W
BLOCK = 128
REL_BUCKETS = 32
REL_MAX_DISTANCE = 2048
CONV_CHANNELS = D_MODEL
CONV_WIDTH = 31
FFN_HIDDEN = 2816
RMS_EPS = 1e-6
LN_EPS = 1e-5
NEG_INF = -1e30

NAT_W = 2 * CONV_CHANNELS + 2 * D_MODEL + QKV_W
NAT_QKV_COL = (2 * CONV_CHANNELS + 2 * D_MODEL) // GROUP_W

LANES = 128
SUBLANES = 8
CONV_ROWS = 64
LSE_W = LANES
HALO = 32
VMEM_LIMIT = 56 * 1024 * 1024

BF16 = jnp.bfloat16
F32 = jnp.float32


def _sigmoid(v):
    return 1.0 / (1.0 + jnp.exp(-v))


def _rms(v, g):
    return v * lax.rsqrt(jnp.mean(v * v, axis=-1, keepdims=True) + RMS_EPS) * g


def _inproj_kernel(*refs, r):
    x_refs, (g_ref, w_ref, o_ref, h_ref) = refs[:-4], refs[-4:]
    tm = x_refs[0].shape[0]
    n = tm // r

    @pl.when(pl.program_id(1) == 0)
    def _():
        if r == 1:
            h_ref[...] = _rms(x_refs[0][...], g_ref[...]).astype(BF16)
        else:
            for c in range(r):
                xs = jnp.concatenate([xr[pl.ds(c, n, stride=r), :] for xr in x_refs], axis=-1)
                h_ref[c * n:(c + 1) * n, :] = _rms(xs, g_ref[...]).astype(BF16)

    z = jnp.dot(h_ref[...], w_ref[...], preferred_element_type=F32).astype(BF16)
    o_ref[...] = z.reshape(o_ref.shape)


def _in_proj(x2, g, w_bf16, r, tm, tn):
    S = x2.shape[0]
    N = w_bf16.shape[1]
    if r == 1:
        x_specs = [pl.BlockSpec((tm, D_MODEL), lambda i, j: (i, 0))]
    else:
        x_specs = [pl.BlockSpec((tm, LANES), lambda i, j, k=k: (i, k)) for k in range(D_MODEL // LANES)]
    return pl.pallas_call(
        functools.partial(_inproj_kernel, r=r),
        grid=(S // tm, N // tn),
        in_specs=x_specs + [
            pl.BlockSpec((1, D_MODEL), lambda i, j: (0, 0)),
            pl.BlockSpec((D_MODEL, tn), lambda i, j: (0, j)),
        ],
        out_specs=pl.BlockSpec((r, tm // r, tn), lambda i, j: (0, i, j)),
        out_shape=jax.ShapeDtypeStruct((r, S // r, N), BF16),
        scratch_shapes=[pltpu.VMEM((tm, D_MODEL), BF16)],
        compiler_params=pltpu.CompilerParams(
            dimension_semantics=("parallel", "arbitrary"), vmem_limit_bytes=VMEM_LIMIT),
        name=f"in_proj_r{r}",
    )(*([x2] * len(x_specs)), g, w_bf16)


def _bucket_maps():
    max_exact = REL_BUCKETS // 2
    a = np.arange(BLOCK, dtype=np.int64)[:, None]
    c = np.arange(2 * BLOCK, dtype=np.int64)[None, :]
    offset = a - c + BLOCK
    maps = []
    for window, dilation in DILATED_GROUPS:
        span = window // dilation
        assert span == BLOCK
        d = np.maximum(offset * dilation, 0)
        df = np.maximum(d, 1).astype(np.float32)
        large = max_exact + (np.log(df / np.float32(max_exact)) / np.float32(math.log(REL_MAX_DISTANCE / max_exact))
                             * np.float32(REL_BUCKETS - max_exact)).astype(np.int32)
        large = np.minimum(large, REL_BUCKETS - 1)
        bucket = np.where(d < max_exact, d, large)
        valid = (offset >= 0) & (offset <= span)
        maps.append(np.where(valid, bucket, -1).astype(np.int32))
    return np.stack(maps)


def _bias_kernel(tab_ref, bucket_ref, o_ref):
    g = pl.program_id(0)
    bucket = bucket_ref[...]
    for h in range(HEADS_PER_GROUP):
        acc = jnp.full(bucket.shape, NEG_INF, F32)
        for b in range(REL_BUCKETS):
            acc = jnp.where(bucket == b, tab_ref[b, g * HEADS_PER_GROUP + h], acc)
        o_ref[h] = acc


def _bias_tables(rel_bias_table):
    return pl.pallas_call(
        _bias_kernel,
        grid=(N_GROUPS,),
        in_specs=[
            pl.BlockSpec(memory_space=pltpu.SMEM),
            pl.BlockSpec((None, BLOCK, 2 * BLOCK), lambda g: (g, 0, 0)),
        ],
        out_specs=pl.BlockSpec((None, HEADS_PER_GROUP, BLOCK, 2 * BLOCK), lambda g: (g, 0, 0, 0)),
        out_shape=jax.ShapeDtypeStruct((N_GROUPS, HEADS_PER_GROUP, BLOCK, 2 * BLOCK), F32),
        name="bias_tables",
    )(rel_bias_table.astype(F32), jnp.asarray(_bucket_maps()))


def _attn_kernel(q_ref, kp_ref, kc_ref, vp_ref, vc_ref, bias_ref, o_ref, lse_ref):
    first = pl.program_id(1) == 0
    lane = lax.broadcasted_iota(jnp.int32, (BLOCK, LSE_W), 1)
    col = lax.broadcasted_iota(jnp.int32, (BLOCK, 2 * BLOCK), 1)
    prev_pad = jnp.where(first & (col < BLOCK), NEG_INF, 0.0).astype(F32)
    lse_tile = jnp.zeros((BLOCK, LSE_W), F32)
    for h in range(HEADS_PER_GROUP):
        sl = slice(h * HEAD_DIM, (h + 1) * HEAD_DIM)
        q = q_ref[:, sl] * jnp.asarray(HEAD_DIM ** -0.5, BF16)
        k = jnp.concatenate([kp_ref[:, sl], kc_ref[:, sl]], axis=0)
        v = jnp.concatenate([vp_ref[:, sl], vc_ref[:, sl]], axis=0)
        s = lax.dot_general(q, k, (((1,), (1,)), ((), ())), preferred_element_type=F32)
        s = s + bias_ref[h] + prev_pad
        m = jnp.max(s, axis=-1, keepdims=True)
        p = jnp.exp(s - m)
        l = jnp.sum(p, axis=-1, keepdims=True)
        o = jnp.dot(p.astype(BF16), v, preferred_element_type=F32) / l
        o_ref[:, sl] = o.astype(o_ref.dtype)
        lse_tile = jnp.where(lane == h, m + jnp.log(l), lse_tile)
    lse_ref[...] = lse_tile


def _attention_group(qkv, qkv_col, biases, g):
    r, L, _ = qkv.shape
    nb = L // BLOCK

    def spec(part, prev):
        if prev:
            return pl.BlockSpec((None, BLOCK, GROUP_W),
                                lambda c, i: (c, jnp.maximum(i - 1, 0), qkv_col + part))
        return pl.BlockSpec((None, BLOCK, GROUP_W), lambda c, i: (c, i, qkv_col + part))

    return pl.pallas_call(
        _attn_kernel,
        grid=(r, nb),
        in_specs=[
            spec(0, False),
            spec(1, True), spec(1, False),
            spec(2, True), spec(2, False),
            pl.BlockSpec((None, HEADS_PER_GROUP, BLOCK, 2 * BLOCK), lambda c, i: (g, 0, 0, 0)),
        ],
        out_specs=[
            pl.BlockSpec((None, BLOCK, GROUP_W), lambda c, i: (c, i, 0)),
            pl.BlockSpec((None, BLOCK, LSE_W), lambda c, i: (c, i, 0)),
        ],
        out_shape=[
            jax.ShapeDtypeStruct((r, L, GROUP_W), BF16),
            jax.ShapeDtypeStruct((r, L, LSE_W), F32),
        ],
        compiler_params=pltpu.CompilerParams(
            dimension_semantics=("parallel", "arbitrary"), vmem_limit_bytes=VMEM_LIMIT),
        name=f"attn_g{g}",
    )(qkv, qkv, qkv, qkv, qkv, biases)


def _causal_dwconv(ext_ref, sh_ref, cv_ref, w_ref, b_ref):
    tm = cv_ref.shape[1]
    ns = sh_ref.shape[1]

    def chunk(k, carry):
        ext = ext_ref.at[k]
        for s in range(1, SUBLANES):
            sh_ref[s - 1] = ext[s:s + ns, :]
        for r0 in range(0, tm, CONV_ROWS):
            acc = jnp.zeros((CONV_ROWS, LANES), F32) + b_ref[k]
            for j in range(CONV_WIDTH):
                a, s = divmod(HALO - (CONV_WIDTH - 1) + j, SUBLANES)
                src = ext if s == 0 else sh_ref.at[s - 1]
                row = a * SUBLANES + r0
                acc = acc + src[row:row + CONV_ROWS, :] * w_ref[k, j:j + 1, :]
            cv_ref[k, r0:r0 + CONV_ROWS, :] = acc
        return carry

    lax.fori_loop(0, ext_ref.shape[0], chunk, 0)


def _mixer_kernel(o0_ref, o1_ref, o2_ref, l0_ref, l1_ref, l2_ref,
                  u_ref, g_ref, hu_ref, hg_ref, ga_ref, gc_ref, x_ref,
                  expand_ref, b_glu_ref, w_dw_ref, b_dw_ref, g_ln_ref, b_ln_ref,
                  w_co_ref, b_co_ref, w_ao_ref, w_mo_ref, g_pm_ref,
                  out_ref, ext_ref, sh_ref, cv_ref, on_ref, ln_ref):
    tm = x_ref.shape[0]
    C = CONV_CHANNELS

    def natural(src_ref, dst_ref):
        r, n, w = src_ref.shape
        if r == 1:
            return src_ref[0].astype(F32)
        for c in range(r):
            for k in range(w // LANES):
                dst_ref[k, pl.ds(c, n, stride=r), :] = src_ref[c, :, k * LANES:(k + 1) * LANES].astype(F32)
        return jnp.concatenate([dst_ref[k] for k in range(w // LANES)], axis=-1)

    l0 = natural(l0_ref, ln_ref.at[0])
    l1 = natural(l1_ref, ln_ref.at[1])
    l2 = natural(l2_ref, ln_ref.at[2])
    lm = jnp.maximum(jnp.maximum(l0, l1), l2)
    e0, e1, e2 = jnp.exp(l0 - lm), jnp.exp(l1 - lm), jnp.exp(l2 - lm)
    inv = 1.0 / (e0 + e1 + e2)
    expand = expand_ref[...]

    def widen(a):
        hi = a.astype(BF16)
        lo = (a - hi.astype(F32)).astype(BF16)
        return (jnp.dot(hi, expand, preferred_element_type=F32)
                + jnp.dot(lo, expand, preferred_element_type=F32))

    o = (widen(e0 * inv) * natural(o0_ref, on_ref.at[0])
         + widen(e1 * inv) * natural(o1_ref, on_ref.at[1])
         + widen(e2 * inv) * natural(o2_ref, on_ref.at[2]))
    y_attn = jnp.dot(o.astype(BF16), w_ao_ref[...], preferred_element_type=F32)

    b_u = b_glu_ref[:, :C]
    b_g = b_glu_ref[:, C:]

    def glu(a_ref, gt_ref):
        return (a_ref[...].astype(F32) + b_u) * _sigmoid(gt_ref[...].astype(F32) + b_g)

    halo = jnp.where(pl.program_id(0) == 0, 0.0, glu(hu_ref, hg_ref))
    cur = glu(u_ref, g_ref)
    for k in range(C // LANES):
        ext_ref[k, 0:HALO, :] = halo[:, k * LANES:(k + 1) * LANES]
        ext_ref[k, HALO:HALO + tm, :] = cur[:, k * LANES:(k + 1) * LANES]
    _causal_dwconv(ext_ref, sh_ref, cv_ref, w_dw_ref, b_dw_ref)
    acc = jnp.concatenate([cv_ref[k] for k in range(C // LANES)], axis=-1)
    mu = jnp.mean(acc, axis=-1, keepdims=True)
    xc = acc - mu
    y = xc * lax.rsqrt(jnp.mean(xc * xc, axis=-1, keepdims=True) + LN_EPS)
    y = y * g_ln_ref[...] + b_ln_ref[...]
    y = y * _sigmoid(y)
    y_conv = jnp.dot(y.astype(BF16), w_co_ref[...], preferred_element_type=F32) + b_co_ref[...]

    merged = (_sigmoid(ga_ref[...].astype(F32)) * y_attn
              + _sigmoid(gc_ref[...].astype(F32)) * y_conv)
    mix = jnp.dot(merged.astype(BF16), w_mo_ref[...], preferred_element_type=F32)
    out_ref[...] = x_ref[...] + _rms(mix, g_pm_ref[...])


def _mixer(x2, z_nat, os_, lses, params, tm=512):
    S = x2.shape[0]
    hb = tm // HALO
    C = CONV_CHANNELS

    def zspec(col):
        return pl.BlockSpec((tm, C), lambda i: (i, col))

    def hspec(col):
        return pl.BlockSpec((HALO, C), lambda i: (jnp.maximum(i * hb - 1, 0), col))

    def full(a):
        return pl.BlockSpec(a.shape, lambda i: (0,) * a.ndim)

    def rspec(a):
        r, _, w = a.shape
        return pl.BlockSpec((r, tm // r, w), lambda i: (0, i, 0))

    return pl.pallas_call(
        _mixer_kernel,
        grid=(S // tm,),
        in_specs=[rspec(a) for a in os_] + [rspec(a) for a in lses]
                 + [zspec(0), zspec(1), hspec(0), hspec(1), zspec(2), zspec(3)]
                 + [pl.BlockSpec((tm, D_MODEL), lambda i: (i, 0))]
                 + [full(p) for p in params],
        out_specs=pl.BlockSpec((tm, D_MODEL), lambda i: (i, 0)),
        out_shape=jax.ShapeDtypeStruct((S, D_MODEL), F32),
        scratch_shapes=[pltpu.VMEM((C // LANES, HALO + tm, LANES), F32),
                        pltpu.VMEM((SUBLANES - 1, HALO + tm - SUBLANES, LANES), F32),
                        pltpu.VMEM((C // LANES, tm, LANES), F32),
                        pltpu.VMEM((N_GROUPS, GROUP_W // LANES, tm, LANES), F32),
                        pltpu.VMEM((N_GROUPS, LSE_W // LANES, tm, LANES), F32)],
        compiler_params=pltpu.CompilerParams(
            dimension_semantics=("parallel",), vmem_limit_bytes=VMEM_LIMIT),
        name="mixer",
    )(*os_, *lses, *([z_nat] * 6), x2, *params)


def _ffn_kernel(x_ref, g_pre_ref, w_in_ref, w_out_ref, g_post_ref, out_ref, *, chunk):
    x = x_ref[...]
    h = _rms(x, g_pre_ref[...]).astype(BF16)
    acc = jnp.zeros(x.shape, F32)
    for c0 in range(0, FFN_HIDDEN, chunk):
        gate = jnp.dot(h, w_in_ref[:, c0:c0 + chunk], preferred_element_type=F32)
        up = jnp.dot(h, w_in_ref[:, FFN_HIDDEN + c0:FFN_HIDDEN + c0 + chunk],
                     preferred_element_type=F32)
        act = (gate * _sigmoid(gate) * up).astype(BF16)
        acc = acc + jnp.dot(act, w_out_ref[c0:c0 + chunk, :], preferred_element_type=F32)
    out_ref[...] = x + _rms(acc, g_post_ref[...])


def _ffn(x1, g_pre, w_in, w_out, g_post, tm=512, chunk=1408):
    S = x1.shape[0]
    row = pl.BlockSpec((tm, D_MODEL), lambda i: (i, 0))

    def full(a):
        return pl.BlockSpec(a.shape, lambda i: (0,) * a.ndim, pipeline_mode=pl.Buffered(1))

    return pl.pallas_call(
        functools.partial(_ffn_kernel, chunk=chunk),
        grid=(S // tm,),
        in_specs=[row, full(g_pre), full(w_in), full(w_out), full(g_post)],
        out_specs=row,
        out_shape=jax.ShapeDtypeStruct((S, D_MODEL), F32),
        compiler_params=pltpu.CompilerParams(
            dimension_semantics=("parallel",), vmem_limit_bytes=VMEM_LIMIT),
        name="ffn",
    )(x1, g_pre, w_in, w_out, g_post)


def _lane_chunks(a):
    return a.reshape(a.shape[0], -1, LANES).transpose(1, 0, 2)


def _split_w_in(w):
    q, k, v, rest = (w[:, :ATTN_W], w[:, ATTN_W:2 * ATTN_W], w[:, 2 * ATTN_W:3 * ATTN_W], w[:, 3 * ATTN_W:])

    def qkv(g):
        sl = slice(g * GROUP_W, (g + 1) * GROUP_W)
        return [q[:, sl], k[:, sl], v[:, sl]]

    w_nat = jnp.concatenate([rest] + qkv(0), axis=1).astype(BF16)
    return w_nat, [jnp.concatenate(qkv(g), axis=1).astype(BF16) for g in range(1, N_GROUPS)]


def kernel(x, rel_bias_table, g_pre_mix, w_in, b_glu, w_dw, b_dw, g_conv_ln, b_conv_ln, w_conv_out, b_conv_out, w_attn_out, w_mix_out, g_post_mix, g_pre_ffn, w_ffn_in, w_ffn_out, g_post_ffn):
    B, S, D = x.shape
    depth = w_in.shape[0]
    expand = jnp.asarray(
        (np.arange(LSE_W)[:, None] == np.arange(GROUP_W)[None, :] // HEAD_DIM), dtype=BF16)
    biases = _bias_tables(rel_bias_table)
    outs = []
    for b in range(B):
        xb = x[b]
        for l in range(depth):
            r2 = lambda a: a[l].reshape(1, -1)
            w_nat, w_dil = _split_w_in(w_in[l])
            g_pre = r2(g_pre_mix)
            z_nat = _in_proj(xb, g_pre, w_nat, 1, tm=1024, tn=1408)
            os_, lses = [], []
            for g, (window, dilation) in enumerate(DILATED_GROUPS):
                if g == 0:
                    qkv, col = z_nat, NAT_QKV_COL
                else:
                    qkv, col = _in_proj(xb, g_pre, w_dil[g - 1], dilation, tm=1024, tn=QKV_W), 0
                o_g, lse_g = _attention_group(qkv, col, biases, g)
                os_.append(o_g)
                lses.append(lse_g)
            params = (expand, r2(b_glu), _lane_chunks(w_dw[l]), _lane_chunks(r2(b_dw)), r2(g_conv_ln), r2(b_conv_ln),
                      w_conv_out[l].astype(BF16), r2(b_conv_out), w_attn_out[l].astype(BF16),
                      w_mix_out[l].astype(BF16), r2(g_post_mix))
            x1 = _mixer(xb, z_nat[0], os_, lses, params)
            xb = _ffn(x1, r2(g_pre_ffn), w_ffn_in[l].astype(BF16), w_ffn_out[l].astype(BF16),
                      r2(g_post_ffn))
        outs.append(xb)
    return jnp.stack(outs, axis=0)
```

```python
import functools
import math

import jax
import jax.numpy as jnp
import numpy as np
from jax import lax
from jax.experimental import pallas as pl
from jax.experimental.pallas import tpu as pltpu

D_MODEL = 1024
HEAD_DIM = 64
HEADS_PER_GROUP = 8
DILATED_GROUPS = ((128, 1), (512, 4), (2048, 16))
N_GROUPS = len(DILATED_GROUPS)
GROUP_W = HEADS_PER_GROUP * HEAD_DIM
ATTN_W = N_GROUPS * GROUP_W
QKV_W = 3 * GROUP_W
BLOCK = 128
REL_BUCKETS = 32
REL_MAX_DISTANCE = 2048
CONV_CHANNELS = D_MODEL
CONV_WIDTH = 31
FFN_HIDDEN = 2816
RMS_EPS = 1e-6
LN_EPS = 1e-5
NEG_INF = -1e30
LOG2E = math.log2(math.e)

NAT_W = 2 * CONV_CHANNELS + 2 * D_MODEL + QKV_W
NAT_QKV_COL = (2 * CONV_CHANNELS + 2 * D_MODEL) // GROUP_W

LANES = 128
SUBLANES = 8
CONV_ROWS = 64
LSE_W = LANES
HALO = 32
VMEM_LIMIT = 56 * 1024 * 1024

BF16 = jnp.bfloat16
F32 = jnp.float32


def _sigmoid(v):
    return 1.0 / (1.0 + jnp.exp(-v))


def _rms(v, g):
    return v * lax.rsqrt(jnp.mean(v * v, axis=-1, keepdims=True) + RMS_EPS) * g


def _inproj_kernel(*refs, r):
    x_refs, (g_ref, w_ref, o_ref, h_ref) = refs[:-4], refs[-4:]
    tm = x_refs[0].shape[0]
    n = tm // r

    @pl.when(pl.program_id(1) == 0)
    def _():
        if r == 1:
            h_ref[...] = _rms(x_refs[0][...], g_ref[...]).astype(BF16)
        else:
            for c in range(r):
                xs = jnp.concatenate([xr[pl.ds(c, n, stride=r), :] for xr in x_refs], axis=-1)
                h_ref[c * n:(c + 1) * n, :] = _rms(xs, g_ref[...]).astype(BF16)

    z = jnp.dot(h_ref[...], w_ref[...], preferred_element_type=F32).astype(BF16)
    o_ref[...] = z.reshape(o_ref.shape)


def _in_proj(x2, g, w_bf16, r, tm, tn):
    S = x2.shape[0]
    N = w_bf16.shape[1]
    if r == 1:
        x_specs = [pl.BlockSpec((tm, D_MODEL), lambda i, j: (i, 0))]
    else:
        x_specs = [pl.BlockSpec((tm, LANES), lambda i, j, k=k: (i, k)) for k in range(D_MODEL // LANES)]
    return pl.pallas_call(
        functools.partial(_inproj_kernel, r=r),
        grid=(S // tm, N // tn),
        in_specs=x_specs + [
            pl.BlockSpec((1, D_MODEL), lambda i, j: (0, 0)),
            pl.BlockSpec((D_MODEL, tn), lambda i, j: (0, j)),
        ],
        out_specs=pl.BlockSpec((r, tm // r, tn), lambda i, j: (0, i, j)),
        out_shape=jax.ShapeDtypeStruct((r, S // r, N), BF16),
        scratch_shapes=[pltpu.VMEM((tm, D_MODEL), BF16)],
        compiler_params=pltpu.CompilerParams(
            dimension_semantics=("parallel", "arbitrary"), vmem_limit_bytes=VMEM_LIMIT),
        name=f"in_proj_r{r}",
    )(*([x2] * len(x_specs)), g, w_bf16)


def _bucket_maps():
    max_exact = REL_BUCKETS // 2
    a = np.arange(BLOCK, dtype=np.int64)[:, None]
    c = np.arange(2 * BLOCK, dtype=np.int64)[None, :]
    offset = a - c + BLOCK
    maps = []
    for window, dilation in DILATED_GROUPS:
        span = window // dilation
        assert span == BLOCK
        d = np.maximum(offset * dilation, 0)
        df = np.maximum(d, 1).astype(np.float32)
        large = max_exact + (np.log(df / np.float32(max_exact)) / np.float32(math.log(REL_MAX_DISTANCE / max_exact))
                             * np.float32(REL_BUCKETS - max_exact)).astype(np.int32)
        large = np.minimum(large, REL_BUCKETS - 1)
        bucket = np.where(d < max_exact, d, large)
        valid = (offset >= 0) & (offset <= span)
        maps.append(np.where(valid, bucket, -1).astype(np.int32))
    return np.stack(maps)


def _bias_kernel(tab_ref, bucket_ref, o_ref):
    g = pl.program_id(0)
    bucket = bucket_ref[...]
    col = lax.broadcasted_iota(jnp.int32, bucket.shape, 1)
    for h in range(HEADS_PER_GROUP):
        acc = jnp.full(bucket.shape, NEG_INF, F32)
        for b in range(REL_BUCKETS):
            acc = jnp.where(bucket == b, tab_ref[b, g * HEADS_PER_GROUP + h] * LOG2E, acc)
        o_ref[0, h] = acc
        o_ref[1, h] = jnp.where(col < BLOCK, NEG_INF, acc)


def _bias_tables(rel_bias_table):
    return pl.pallas_call(
        _bias_kernel,
        grid=(N_GROUPS,),
        in_specs=[
            pl.BlockSpec(memory_space=pltpu.SMEM),
            pl.BlockSpec((None, BLOCK, 2 * BLOCK), lambda g: (g, 0, 0)),
        ],
        out_specs=pl.BlockSpec((None, 2, HEADS_PER_GROUP, BLOCK, 2 * BLOCK), lambda g: (g, 0, 0, 0, 0)),
        out_shape=jax.ShapeDtypeStruct((N_GROUPS, 2, HEADS_PER_GROUP, BLOCK, 2 * BLOCK), F32),
        name="bias_tables",
    )(rel_bias_table.astype(F32), jnp.asarray(_bucket_maps()))


def _attn_kernel(q_ref, kp_ref, kc_ref, vp_ref, vc_ref, bias_ref, sumcol_ref, o_ref, ml_ref):
    variant = jnp.where(pl.program_id(1) == 0, 1, 0)
    lane = lax.broadcasted_iota(jnp.int32, (BLOCK, LANES), 1)
    low = lane < HEAD_DIM
    ml = jnp.zeros((BLOCK, LANES), F32)
    for pair in range(HEADS_PER_GROUP // 2):
        sl = slice(pair * LANES, (pair + 1) * LANES)
        q = q_ref[:, sl] * jnp.asarray(HEAD_DIM ** -0.5, BF16)
        k = jnp.concatenate([kp_ref[:, sl], kc_ref[:, sl]], axis=0)
        v = jnp.concatenate([vp_ref[:, sl], vc_ref[:, sl]], axis=0)
        halves = []
        for sub in range(2):
            h = 2 * pair + sub
            qm = jnp.where(low if sub == 0 else ~low, q, jnp.zeros_like(q))
            s = lax.dot_general(qm, k, (((1,), (1,)), ((), ())), preferred_element_type=F32)
            t = s * LOG2E + bias_ref[variant, h]
            mt = jnp.max(t, axis=-1, keepdims=True)
            p = jnp.exp2(t - mt).astype(BF16)
            r = jnp.dot(p, jnp.concatenate([v, sumcol_ref[h]], axis=1), preferred_element_type=F32)
            halves.append(r[:, :LANES])
            ml = jnp.where(lane == h, mt, ml) + r[:, LANES:]
        o_ref[:, sl] = jnp.where(low, halves[0], halves[1]).astype(o_ref.dtype)
    ml_ref[...] = ml


def _attention_group(qkv, qkv_col, biases, g):
    r, L, _ = qkv.shape
    nb = L // BLOCK
    sumcol = jnp.asarray(np.broadcast_to(
        np.arange(LANES)[None, None, :] == HEADS_PER_GROUP + np.arange(HEADS_PER_GROUP)[:, None, None],
        (HEADS_PER_GROUP, 2 * BLOCK, LANES)), dtype=BF16)

    def spec(part, prev):
        if prev:
            return pl.BlockSpec((None, BLOCK, GROUP_W),
                                lambda c, i: (c, jnp.maximum(i - 1, 0), qkv_col + part))
        return pl.BlockSpec((None, BLOCK, GROUP_W), lambda c, i: (c, i, qkv_col + part))

    return pl.pallas_call(
        _attn_kernel,
        grid=(r, nb),
        in_specs=[
            spec(0, False),
            spec(1, True), spec(1, False),
            spec(2, True), spec(2, False),
            pl.BlockSpec((None, 2, HEADS_PER_GROUP, BLOCK, 2 * BLOCK), lambda c, i: (g, 0, 0, 0, 0)),
            pl.BlockSpec(sumcol.shape, lambda c, i: (0, 0, 0)),
        ],
        out_specs=[
            pl.BlockSpec((None, BLOCK, GROUP_W), lambda c, i: (c, i, 0)),
            pl.BlockSpec((None, BLOCK, LSE_W), lambda c, i: (c, i, 0)),
        ],
        out_shape=[
            jax.ShapeDtypeStruct((r, L, GROUP_W), BF16),
            jax.ShapeDtypeStruct((r, L, LSE_W), F32),
        ],
        compiler_params=pltpu.CompilerParams(
            dimension_semantics=("parallel", "arbitrary"), vmem_limit_bytes=VMEM_LIMIT),
        name=f"attn_g{g}",
    )(qkv, qkv, qkv, qkv, qkv, biases, sumcol)


def _causal_dwconv(ext_ref, sh_ref, cv_ref, w_ref, b_ref):
    tm = cv_ref.shape[1]
    ns = sh_ref.shape[1]

    def chunk(k, carry):
        ext = ext_ref.at[k]
        for s in range(1, SUBLANES):
            sh_ref[s - 1] = ext[s:s + ns, :]
        for r0 in range(0, tm, CONV_ROWS):
            acc = jnp.zeros((CONV_ROWS, LANES), F32) + b_ref[k]
            for j in range(CONV_WIDTH):
                a, s = divmod(HALO - (CONV_WIDTH - 1) + j, SUBLANES)
                src = ext if s == 0 else sh_ref.at[s - 1]
                row = a * SUBLANES + r0
                acc = acc + src[row:row + CONV_ROWS, :] * w_ref[k, j:j + 1, :]
            cv_ref[k, r0:r0 + CONV_ROWS, :] = acc
        return carry

    lax.fori_loop(0, ext_ref.shape[0], chunk, 0)


def _mixer_kernel(o0_ref, o1_ref, o2_ref, l0_ref, l1_ref, l2_ref,
                  u_ref, g_ref, hu_ref, hg_ref, ga_ref, gc_ref, x_ref,
                  expand_ref, b_glu_ref, w_dw_ref, b_dw_ref, g_ln_ref, b_ln_ref,
                  w_co_ref, b_co_ref, w_ao_ref, w_mo_ref, g_pm_ref,
                  out_ref, ext_ref, sh_ref, cv_ref, on_ref, ln_ref):
    tm = x_ref.shape[0]
    C = CONV_CHANNELS

    def natural(src_ref, dst_ref):
        r, n, w = src_ref.shape
        if r == 1:
            return src_ref[0].astype(F32)
        for c in range(r):
            for k in range(w // LANES):
                dst_ref[k, pl.ds(c, n, stride=r), :] = src_ref[c, :, k * LANES:(k + 1) * LANES].astype(F32)
        return jnp.concatenate([dst_ref[k] for k in range(w // LANES)], axis=-1)

    ml = [natural(r_, ln_ref.at[g]) for g, r_ in enumerate((l0_ref, l1_ref, l2_ref))]
    den = [pltpu.roll(a, LANES - HEADS_PER_GROUP, axis=1) for a in ml]
    mx = jnp.maximum(jnp.maximum(ml[0], ml[1]), ml[2])
    w = [jnp.exp2(a - mx) for a in ml]
    total = w[0] * den[0] + w[1] * den[1] + w[2] * den[2]
    head_lane = lax.broadcasted_iota(jnp.int32, total.shape, 1) < HEADS_PER_GROUP
    inv = jnp.where(head_lane, 1.0 / total, 0.0)
    expand = expand_ref[...]

    def widen(a):
        hi = a.astype(BF16)
        lo = (a - hi.astype(F32)).astype(BF16)
        return (jnp.dot(hi, expand, preferred_element_type=F32)
                + jnp.dot(lo, expand, preferred_element_type=F32))

    o = (widen(w[0] * inv) * natural(o0_ref, on_ref.at[0])
         + widen(w[1] * inv) * natural(o1_ref, on_ref.at[1])
         + widen(w[2] * inv) * natural(o2_ref, on_ref.at[2]))
    y_attn = jnp.dot(o.astype(BF16), w_ao_ref[...], preferred_element_type=F32)

    b_u = b_glu_ref[:, :C]
    b_g = b_glu_ref[:, C:]

    def glu(a_ref, gt_ref):
        return (a_ref[...].astype(F32) + b_u) * _sigmoid(gt_ref[...].astype(F32) + b_g)

    halo = jnp.where(pl.program_id(0) == 0, 0.0, glu(hu_ref, hg_ref))
    cur = glu(u_ref, g_ref)
    for k in range(C // LANES):
        ext_ref[k, 0:HALO, :] = halo[:, k * LANES:(k + 1) * LANES]
        ext_ref[k, HALO:HALO + tm, :] = cur[:, k * LANES:(k + 1) * LANES]
    _causal_dwconv(ext_ref, sh_ref, cv_ref, w_dw_ref, b_dw_ref)
    acc = jnp.concatenate([cv_ref[k] for k in range(C // LANES)], axis=-1)
    mu = jnp.mean(acc, axis=-1, keepdims=True)
    xc = acc - mu
    y = xc * lax.rsqrt(jnp.mean(xc * xc, axis=-1, keepdims=True) + LN_EPS)
    y = y * g_ln_ref[...] + b_ln_ref[...]
    y = y * _sigmoid(y)
    y_conv = jnp.dot(y.astype(BF16), w_co_ref[...], preferred_element_type=F32) + b_co_ref[...]

    merged = (_sigmoid(ga_ref[...].astype(F32)) * y_attn
              + _sigmoid(gc_ref[...].astype(F32)) * y_conv)
    mix = jnp.dot(merged.astype(BF16), w_mo_ref[...], preferred_element_type=F32)
    out_ref[...] = x_ref[...] + _rms(mix, g_pm_ref[...])


def _mixer(x2, z_nat, os_, lses, params, tm=512):
    S = x2.shape[0]
    hb = tm // HALO
    C = CONV_CHANNELS

    def zspec(col):
        return pl.BlockSpec((tm, C), lambda i: (i, col))

    def hspec(col):
        return pl.BlockSpec((HALO, C), lambda i: (jnp.maximum(i * hb - 1, 0), col))

    def full(a):
        return pl.BlockSpec(a.shape, lambda i: (0,) * a.ndim)

    def rspec(a):
        r, _, w = a.shape
        return pl.BlockSpec((r, tm // r, w), lambda i: (0, i, 0))

    return pl.pallas_call(
        _mixer_kernel,
        grid=(S // tm,),
        in_specs=[rspec(a) for a in os_] + [rspec(a) for a in lses]
                 + [zspec(0), zspec(1), hspec(0), hspec(1), zspec(2), zspec(3)]
                 + [pl.BlockSpec((tm, D_MODEL), lambda i: (i, 0))]
                 + [full(p) for p in params],
        out_specs=pl.BlockSpec((tm, D_MODEL), lambda i: (i, 0)),
        out_shape=jax.ShapeDtypeStruct((S, D_MODEL), F32),
        scratch_shapes=[pltpu.VMEM((C // LANES, HALO + tm, LANES), F32),
                        pltpu.VMEM((SUBLANES - 1, HALO + tm - SUBLANES, LANES), F32),
                        pltpu.VMEM((C // LANES, tm, LANES), F32),
                        pltpu.VMEM((N_GROUPS, GROUP_W // LANES, tm, LANES), F32),
                        pltpu.VMEM((N_GROUPS, LSE_W // LANES, tm, LANES), F32)],
        compiler_params=pltpu.CompilerParams(
            dimension_semantics=("parallel",), vmem_limit_bytes=VMEM_LIMIT),
        name="mixer",
    )(*os_, *lses, *([z_nat] * 6), x2, *params)


def _ffn_kernel(x_ref, g_pre_ref, w_in_ref, w_out_ref, g_post_ref, out_ref, *, chunk):
    x = x_ref[...]
    h = _rms(x, g_pre_ref[...]).astype(BF16)
    acc = jnp.zeros(x.shape, F32)
    for c0 in range(0, FFN_HIDDEN, chunk):
        gate = jnp.dot(h, w_in_ref[:, c0:c0 + chunk], preferred_element_type=F32)
        up = jnp.dot(h, w_in_ref[:, FFN_HIDDEN + c0:FFN_HIDDEN + c0 + chunk],
                     preferred_element_type=F32)
        act = (gate * _sigmoid(gate) * up).astype(BF16)
        acc = acc + jnp.dot(act, w_out_ref[c0:c0 + chunk, :], preferred_element_type=F32)
    out_ref[...] = x + _rms(acc, g_post_ref[...])


def _ffn(x1, g_pre, w_in, w_out, g_post, tm=512, chunk=1408):
    S = x1.shape[0]
    row = pl.BlockSpec((tm, D_MODEL), lambda i: (i, 0))

    def full(a):
        return pl.BlockSpec(a.shape, lambda i: (0,) * a.ndim, pipeline_mode=pl.Buffered(1))

    return pl.pallas_call(
        functools.partial(_ffn_kernel, chunk=chunk),
        grid=(S // tm,),
        in_specs=[row, full(g_pre), full(w_in), full(w_out), full(g_post)],
        out_specs=row,
        out_shape=jax.ShapeDtypeStruct((S, D_MODEL), F32),
        compiler_params=pltpu.CompilerParams(
            dimension_semantics=("parallel",), vmem_limit_bytes=VMEM_LIMIT),
        name="ffn",
    )(x1, g_pre, w_in, w_out, g_post)


def _lane_chunks(a):
    return a.reshape(a.shape[0], -1, LANES).transpose(1, 0, 2)


def _split_w_in(w):
    q, k, v, rest = (w[:, :ATTN_W], w[:, ATTN_W:2 * ATTN_W], w[:, 2 * ATTN_W:3 * ATTN_W], w[:, 3 * ATTN_W:])

    def qkv(g):
        sl = slice(g * GROUP_W, (g + 1) * GROUP_W)
        return [q[:, sl], k[:, sl], v[:, sl]]

    w_nat = jnp.concatenate([rest] + qkv(0), axis=1).astype(BF16)
    return w_nat, [jnp.concatenate(qkv(g), axis=1).astype(BF16) for g in range(1, N_GROUPS)]


def kernel(x, rel_bias_table, g_pre_mix, w_in, b_glu, w_dw, b_dw, g_conv_ln, b_conv_ln, w_conv_out, b_conv_out, w_attn_out, w_mix_out, g_post_mix, g_pre_ffn, w_ffn_in, w_ffn_out, g_post_ffn):
    B, S, D = x.shape
    depth = w_in.shape[0]
    expand = jnp.asarray(
        (np.arange(LSE_W)[:, None] == np.arange(GROUP_W)[None, :] // HEAD_DIM), dtype=BF16)
    biases = _bias_tables(rel_bias_table)
    outs = []
    for b in range(B):
        xb = x[b]
        for l in range(depth):
            r2 = lambda a: a[l].reshape(1, -1)
            w_nat, w_dil = _split_w_in(w_in[l])
            g_pre = r2(g_pre_mix)
            z_nat = _in_proj(xb, g_pre, w_nat, 1, tm=1024, tn=1408)
            os_, lses = [], []
            for g, (window, dilation) in enumerate(DILATED_GROUPS):
                if g == 0:
                    qkv, col = z_nat, NAT_QKV_COL
                else:
                    qkv, col = _in_proj(xb, g_pre, w_dil[g - 1], dilation, tm=1024, tn=QKV_W), 0
                o_g, lse_g = _attention_group(qkv, col, biases, g)
                os_.append(o_g)
                lses.append(lse_g)
            params = (expand, r2(b_glu), _lane_chunks(w_dw[l]), _lane_chunks(r2(b_dw)), r2(g_conv_ln), r2(b_conv_ln),
                      w_conv_out[l].astype(BF16), r2(b_conv_out), w_attn_out[l].astype(BF16),
                      w_mix_out[l].astype(BF16), r2(g_post_mix))
            x1 = _mixer(xb, z_nat[0], os_, lses, params)
            xb = _ffn(x1, r2(g_pre_ffn), w_ffn_in[l].astype(BF16), w_ffn_out[l].astype(BF16),
                      r2(g_post_ffn))
        outs.append(xb)
    return jnp.stack(outs, axis=0)
```

```python
import functools
import math

import jax
import jax.numpy as jnp
import numpy as np
from jax import lax
from jax.experimental import pallas as pl
from jax.experimental.pallas import tpu as pltpu

D_MODEL = 1024
HEAD_DIM = 64
HEADS_PER_GROUP = 8
DILATED_GROUPS = ((128, 1), (512, 4), (2048, 16))
N_GROUPS = len(DILATED_GROUPS)
GROUP_W = HEADS_PER_GROUP * HEAD_DIM
ATTN_W = N_GROUPS * GROUP_W
QKV_W = 3 * GROUP_W
BLOCK = 128
REL_BUCKETS = 32
REL_MAX_DISTANCE = 2048
CONV_CHANNELS = D_MODEL
CONV_WIDTH = 31
FFN_HIDDEN = 2816
RMS_EPS = 1e-6
LN_EPS = 1e-5
NEG_INF = -1e30
LOG2E = math.log2(math.e)

NAT_W = 2 * CONV_CHANNELS + 2 * D_MODEL + QKV_W
NAT_QKV_COL = (2 * CONV_CHANNELS + 2 * D_MODEL) // GROUP_W

LANES = 128
SUBLANES = 8
CONV_ROWS = 64
LSE_W = LANES
HALO = 32
VMEM_LIMIT = 56 * 1024 * 1024

BF16 = jnp.bfloat16
F32 = jnp.float32


def _sigmoid(v):
    return 1.0 / (1.0 + jnp.exp(-v))


def _rms(v, g):
    return v * lax.rsqrt(jnp.mean(v * v, axis=-1, keepdims=True) + RMS_EPS) * g


def _inproj_kernel(*refs, r):
    x_refs, (g_ref, w_ref, o_ref, h_ref) = refs[:-4], refs[-4:]
    tm = x_refs[0].shape[0]
    n = tm // r

    @pl.when(pl.program_id(1) == 0)
    def _():
        if r == 1:
            h_ref[...] = _rms(x_refs[0][...], g_ref[...]).astype(BF16)
        else:
            for c in range(r):
                xs = jnp.concatenate([xr[pl.ds(c, n, stride=r), :] for xr in x_refs], axis=-1)
                h_ref[c * n:(c + 1) * n, :] = _rms(xs, g_ref[...]).astype(BF16)

    z = jnp.dot(h_ref[...], w_ref[...], preferred_element_type=F32).astype(BF16)
    o_ref[...] = z.reshape(o_ref.shape)


def _in_proj(x2, g, w_bf16, r, tm, tn):
    S = x2.shape[0]
    N = w_bf16.shape[1]
    if r == 1:
        x_specs = [pl.BlockSpec((tm, D_MODEL), lambda i, j: (i, 0))]
    else:
        x_specs = [pl.BlockSpec((tm, LANES), lambda i, j, k=k: (i, k)) for k in range(D_MODEL // LANES)]
    return pl.pallas_call(
        functools.partial(_inproj_kernel, r=r),
        grid=(S // tm, N // tn),
        in_specs=x_specs + [
            pl.BlockSpec((1, D_MODEL), lambda i, j: (0, 0)),
            pl.BlockSpec((D_MODEL, tn), lambda i, j: (0, j)),
        ],
        out_specs=pl.BlockSpec((r, tm // r, tn), lambda i, j: (0, i, j)),
        out_shape=jax.ShapeDtypeStruct((r, S // r, N), BF16),
        scratch_shapes=[pltpu.VMEM((tm, D_MODEL), BF16)],
        compiler_params=pltpu.CompilerParams(
            dimension_semantics=("parallel", "arbitrary"), vmem_limit_bytes=VMEM_LIMIT),
        name=f"in_proj_r{r}",
    )(*([x2] * len(x_specs)), g, w_bf16)


def _bucket_maps():
    max_exact = REL_BUCKETS // 2
    a = np.arange(BLOCK, dtype=np.int64)[:, None]
    c = np.arange(2 * BLOCK, dtype=np.int64)[None, :]
    offset = a - c + BLOCK
    maps = []
    for window, dilation in DILATED_GROUPS:
        span = window // dilation
        assert span == BLOCK
        d = np.maximum(offset * dilation, 0)
        df = np.maximum(d, 1).astype(np.float32)
        large = max_exact + (np.log(df / np.float32(max_exact)) / np.float32(math.log(REL_MAX_DISTANCE / max_exact))
                             * np.float32(REL_BUCKETS - max_exact)).astype(np.int32)
        large = np.minimum(large, REL_BUCKETS - 1)
        bucket = np.where(d < max_exact, d, large)
        valid = (offset >= 0) & (offset <= span)
        maps.append(np.where(valid, bucket, -1).astype(np.int32))
    return np.stack(maps)


def _bias_kernel(tab_ref, bucket_ref, o_ref):
    g = pl.program_id(0)
    bucket = bucket_ref[...]
    col = lax.broadcasted_iota(jnp.int32, bucket.shape, 1)
    for h in range(HEADS_PER_GROUP):
        acc = jnp.full(bucket.shape, NEG_INF, F32)
        for b in range(REL_BUCKETS):
            acc = jnp.where(bucket == b, tab_ref[b, g * HEADS_PER_GROUP + h] * LOG2E, acc)
        o_ref[0, h] = acc
        o_ref[1, h] = jnp.where(col < BLOCK, NEG_INF, acc)


def _bias_tables(rel_bias_table):
    return pl.pallas_call(
        _bias_kernel,
        grid=(N_GROUPS,),
        in_specs=[
            pl.BlockSpec(memory_space=pltpu.SMEM),
            pl.BlockSpec((None, BLOCK, 2 * BLOCK), lambda g: (g, 0, 0)),
        ],
        out_specs=pl.BlockSpec((None, 2, HEADS_PER_GROUP, BLOCK, 2 * BLOCK), lambda g: (g, 0, 0, 0, 0)),
        out_shape=jax.ShapeDtypeStruct((N_GROUPS, 2, HEADS_PER_GROUP, BLOCK, 2 * BLOCK), F32),
        name="bias_tables",
    )(rel_bias_table.astype(F32), jnp.asarray(_bucket_maps()))


def _attn_kernel(q_ref, kp_ref, kc_ref, vp_ref, vc_ref, bias_ref, sumcol_ref, o_ref, ml_ref):
    first = jnp.where(pl.program_id(1) == 0, 1, 0)
    lane = lax.broadcasted_iota(jnp.int32, (BLOCK, LANES), 1)
    low = lane < HEAD_DIM
    for j in range(q_ref.shape[0] // BLOCK):
        rows = slice(j * BLOCK, (j + 1) * BLOCK)
        variant = first if j == 0 else 0
        ml = jnp.zeros((BLOCK, LANES), F32)
        for pair in range(HEADS_PER_GROUP // 2):
            sl = slice(pair * LANES, (pair + 1) * LANES)
            q = q_ref[rows, sl] * jnp.asarray(HEAD_DIM ** -0.5, BF16)
            if j == 0:
                k = jnp.concatenate([kp_ref[:, sl], kc_ref[rows, sl]], axis=0)
                v = jnp.concatenate([vp_ref[:, sl], vc_ref[rows, sl]], axis=0)
            else:
                k = kc_ref[(j - 1) * BLOCK:(j + 1) * BLOCK, sl]
                v = vc_ref[(j - 1) * BLOCK:(j + 1) * BLOCK, sl]
            halves = []
            for sub in range(2):
                h = 2 * pair + sub
                qm = jnp.where(low if sub == 0 else ~low, q, jnp.zeros_like(q))
                s = lax.dot_general(qm, k, (((1,), (1,)), ((), ())), preferred_element_type=F32)
                t = s * LOG2E + bias_ref[variant, h]
                mt = jnp.max(t, axis=-1, keepdims=True)
                p = jnp.exp2(t - mt).astype(BF16)
                r = jnp.dot(p, jnp.concatenate([v, sumcol_ref[h]], axis=1), preferred_element_type=F32)
                halves.append(r[:, :LANES])
                ml = jnp.where(lane == h, mt, ml) + r[:, LANES:]
            o_ref[rows, sl] = jnp.where(low, halves[0], halves[1]).astype(o_ref.dtype)
        ml_ref[rows, :] = ml


def _attention_group(qkv, qkv_col, biases, g, nq=4):
    r, L, _ = qkv.shape
    tq = nq * BLOCK
    sumcol = jnp.asarray(np.broadcast_to(
        np.arange(LANES)[None, None, :] == HEADS_PER_GROUP + np.arange(HEADS_PER_GROUP)[:, None, None],
        (HEADS_PER_GROUP, 2 * BLOCK, LANES)), dtype=BF16)

    def spec(part, prev):
        if prev:
            return pl.BlockSpec((None, BLOCK, GROUP_W),
                                lambda c, i: (c, jnp.maximum(i * nq - 1, 0), qkv_col + part))
        return pl.BlockSpec((None, tq, GROUP_W), lambda c, i: (c, i, qkv_col + part))

    return pl.pallas_call(
        _attn_kernel,
        grid=(r, L // tq),
        in_specs=[
            spec(0, False),
            spec(1, True), spec(1, False),
            spec(2, True), spec(2, False),
            pl.BlockSpec((None, 2, HEADS_PER_GROUP, BLOCK, 2 * BLOCK), lambda c, i: (g, 0, 0, 0, 0)),
            pl.BlockSpec(sumcol.shape, lambda c, i: (0, 0, 0)),
        ],
        out_specs=[
            pl.BlockSpec((None, tq, GROUP_W), lambda c, i: (c, i, 0)),
            pl.BlockSpec((None, tq, LSE_W), lambda c, i: (c, i, 0)),
        ],
        out_shape=[
            jax.ShapeDtypeStruct((r, L, GROUP_W), BF16),
            jax.ShapeDtypeStruct((r, L, LSE_W), F32),
        ],
        compiler_params=pltpu.CompilerParams(
            dimension_semantics=("parallel", "arbitrary"), vmem_limit_bytes=VMEM_LIMIT),
        name=f"attn_g{g}",
    )(qkv, qkv, qkv, qkv, qkv, biases, sumcol)


def _causal_dwconv(ext_ref, sh_ref, cv_ref, w_ref, b_ref):
    tm = cv_ref.shape[1]
    ns = sh_ref.shape[1]

    def chunk(k, carry):
        ext = ext_ref.at[k]
        for s in range(1, SUBLANES):
            sh_ref[s - 1] = ext[s:s + ns, :]
        for r0 in range(0, tm, CONV_ROWS):
            acc = jnp.zeros((CONV_ROWS, LANES), F32) + b_ref[k]
            for j in range(CONV_WIDTH):
                a, s = divmod(HALO - (CONV_WIDTH - 1) + j, SUBLANES)
                src = ext if s == 0 else sh_ref.at[s - 1]
                row = a * SUBLANES + r0
                acc = acc + src[row:row + CONV_ROWS, :] * w_ref[k, j:j + 1, :]
            cv_ref[k, r0:r0 + CONV_ROWS, :] = acc
        return carry

    lax.fori_loop(0, ext_ref.shape[0], chunk, 0)


def _mixer_kernel(o0_ref, o1_ref, o2_ref, l0_ref, l1_ref, l2_ref,
                  u_ref, g_ref, hu_ref, hg_ref, ga_ref, gc_ref, x_ref,
                  expand_ref, b_glu_ref, w_dw_ref, b_dw_ref, g_ln_ref, b_ln_ref,
                  w_co_ref, b_co_ref, w_ao_ref, w_mo_ref, g_pm_ref,
                  out_ref, ext_ref, sh_ref, cv_ref, on_ref, ln_ref):
    tm = x_ref.shape[0]
    C = CONV_CHANNELS

    def natural(src_ref, dst_ref):
        r, n, w = src_ref.shape
        if r == 1:
            return src_ref[0].astype(F32)
        for c in range(r):
            for k in range(w // LANES):
                dst_ref[k, pl.ds(c, n, stride=r), :] = src_ref[c, :, k * LANES:(k + 1) * LANES].astype(F32)
        return jnp.concatenate([dst_ref[k] for k in range(w // LANES)], axis=-1)

    ml = [natural(r_, ln_ref.at[g]) for g, r_ in enumerate((l0_ref, l1_ref, l2_ref))]
    den = [pltpu.roll(a, LANES - HEADS_PER_GROUP, axis=1) for a in ml]
    mx = jnp.maximum(jnp.maximum(ml[0], ml[1]), ml[2])
    w = [jnp.exp2(a - mx) for a in ml]
    total = w[0] * den[0] + w[1] * den[1] + w[2] * den[2]
    head_lane = lax.broadcasted_iota(jnp.int32, total.shape, 1) < HEADS_PER_GROUP
    inv = jnp.where(head_lane, 1.0 / total, 0.0)
    expand = expand_ref[...]

    def widen(a):
        hi = a.astype(BF16)
        lo = (a - hi.astype(F32)).astype(BF16)
        return (jnp.dot(hi, expand, preferred_element_type=F32)
                + jnp.dot(lo, expand, preferred_element_type=F32))

    o = (widen(w[0] * inv) * natural(o0_ref, on_ref.at[0])
         + widen(w[1] * inv) * natural(o1_ref, on_ref.at[1])
         + widen(w[2] * inv) * natural(o2_ref, on_ref.at[2]))
    y_attn = jnp.dot(o.astype(BF16), w_ao_ref[...], preferred_element_type=F32)

    b_u = b_glu_ref[:, :C]
    b_g = b_glu_ref[:, C:]

    def glu(a_ref, gt_ref):
        return (a_ref[...].astype(F32) + b_u) * _sigmoid(gt_ref[...].astype(F32) + b_g)

    halo = jnp.where(pl.program_id(0) == 0, 0.0, glu(hu_ref, hg_ref))
    cur = glu(u_ref, g_ref)
    for k in range(C // LANES):
        ext_ref[k, 0:HALO, :] = halo[:, k * LANES:(k + 1) * LANES]
        ext_ref[k, HALO:HALO + tm, :] = cur[:, k * LANES:(k + 1) * LANES]
    _causal_dwconv(ext_ref, sh_ref, cv_ref, w_dw_ref, b_dw_ref)
    acc = jnp.concatenate([cv_ref[k] for k in range(C // LANES)], axis=-1)
    mu = jnp.mean(acc, axis=-1, keepdims=True)
    xc = acc - mu
    y = xc * lax.rsqrt(jnp.mean(xc * xc, axis=-1, keepdims=True) + LN_EPS)
    y = y * g_ln_ref[...] + b_ln_ref[...]
    y = y * _sigmoid(y)
    y_conv = jnp.dot(y.astype(BF16), w_co_ref[...], preferred_element_type=F32) + b_co_ref[...]

    merged = (_sigmoid(ga_ref[...].astype(F32)) * y_attn
              + _sigmoid(gc_ref[...].astype(F32)) * y_conv)
    mix = jnp.dot(merged.astype(BF16), w_mo_ref[...], preferred_element_type=F32)
    out_ref[...] = x_ref[...] + _rms(mix, g_pm_ref[...])


def _mixer(x2, z_nat, os_, lses, params, tm=512):
    S = x2.shape[0]
    hb = tm // HALO
    C = CONV_CHANNELS

    def zspec(col):
        return pl.BlockSpec((tm, C), lambda i: (i, col))

    def hspec(col):
        return pl.BlockSpec((HALO, C), lambda i: (jnp.maximum(i * hb - 1, 0), col))

    def full(a):
        return pl.BlockSpec(a.shape, lambda i: (0,) * a.ndim)

    def rspec(a):
        r, _, w = a.shape
        return pl.BlockSpec((r, tm // r, w), lambda i: (0, i, 0))

    return pl.pallas_call(
        _mixer_kernel,
        grid=(S // tm,),
        in_specs=[rspec(a) for a in os_] + [rspec(a) for a in lses]
                 + [zspec(0), zspec(1), hspec(0), hspec(1), zspec(2), zspec(3)]
                 + [pl.BlockSpec((tm, D_MODEL), lambda i: (i, 0))]
                 + [full(p) for p in params],
        out_specs=pl.BlockSpec((tm, D_MODEL), lambda i: (i, 0)),
        out_shape=jax.ShapeDtypeStruct((S, D_MODEL), F32),
        scratch_shapes=[pltpu.VMEM((C // LANES, HALO + tm, LANES), F32),
                        pltpu.VMEM((SUBLANES - 1, HALO + tm - SUBLANES, LANES), F32),
                        pltpu.VMEM((C // LANES, tm, LANES), F32),
                        pltpu.VMEM((N_GROUPS, GROUP_W // LANES, tm, LANES), F32),
                        pltpu.VMEM((N_GROUPS, LSE_W // LANES, tm, LANES), F32)],
        compiler_params=pltpu.CompilerParams(
            dimension_semantics=("parallel",), vmem_limit_bytes=VMEM_LIMIT),
        name="mixer",
    )(*os_, *lses, *([z_nat] * 6), x2, *params)


def _ffn_kernel(x_ref, g_pre_ref, w_in_ref, w_out_ref, g_post_ref, out_ref, *, chunk):
    x = x_ref[...]
    h = _rms(x, g_pre_ref[...]).astype(BF16)
    acc = jnp.zeros(x.shape, F32)
    for c0 in range(0, FFN_HIDDEN, chunk):
        gate = jnp.dot(h, w_in_ref[:, c0:c0 + chunk], preferred_element_type=F32)
        up = jnp.dot(h, w_in_ref[:, FFN_HIDDEN + c0:FFN_HIDDEN + c0 + chunk],
                     preferred_element_type=F32)
        act = (gate * _sigmoid(gate) * up).astype(BF16)
        acc = acc + jnp.dot(act, w_out_ref[c0:c0 + chunk, :], preferred_element_type=F32)
    out_ref[...] = x + _rms(acc, g_post_ref[...])


def _ffn(x1, g_pre, w_in, w_out, g_post, tm=512, chunk=1408):
    S = x1.shape[0]
    row = pl.BlockSpec((tm, D_MODEL), lambda i: (i, 0))

    def full(a):
        return pl.BlockSpec(a.shape, lambda i: (0,) * a.ndim, pipeline_mode=pl.Buffered(1))

    return pl.pallas_call(
        functools.partial(_ffn_kernel, chunk=chunk),
        grid=(S // tm,),
        in_specs=[row, full(g_pre), full(w_in), full(w_out), full(g_post)],
        out_specs=row,
        out_shape=jax.ShapeDtypeStruct((S, D_MODEL), F32),
        compiler_params=pltpu.CompilerParams(
            dimension_semantics=("parallel",), vmem_limit_bytes=VMEM_LIMIT),
        name="ffn",
    )(x1, g_pre, w_in, w_out, g_post)


def _lane_chunks(a):
    return a.reshape(a.shape[0], -1, LANES).transpose(1, 0, 2)


def _split_w_in(w):
    q, k, v, rest = (w[:, :ATTN_W], w[:, ATTN_W:2 * ATTN_W], w[:, 2 * ATTN_W:3 * ATTN_W], w[:, 3 * ATTN_W:])

    def qkv(g):
        sl = slice(g * GROUP_W, (g + 1) * GROUP_W)
        return [q[:, sl], k[:, sl], v[:, sl]]

    w_nat = jnp.concatenate([rest] + qkv(0), axis=1).astype(BF16)
    return w_nat, [jnp.concatenate(qkv(g), axis=1).astype(BF16) for g in range(1, N_GROUPS)]


def kernel(x, rel_bias_table, g_pre_mix, w_in, b_glu, w_dw, b_dw, g_conv_ln, b_conv_ln, w_conv_out, b_conv_out, w_attn_out, w_mix_out, g_post_mix, g_pre_ffn, w_ffn_in, w_ffn_out, g_post_ffn):
    B, S, D = x.shape
    depth = w_in.shape[0]
    expand = jnp.asarray(
        (np.arange(LSE_W)[:, None] == np.arange(GROUP_W)[None, :] // HEAD_DIM), dtype=BF16)
    biases = _bias_tables(rel_bias_table)
    outs = []
    for b in range(B):
        xb = x[b]
        for l in range(depth):
            r2 = lambda a: a[l].reshape(1, -1)
            w_nat, w_dil = _split_w_in(w_in[l])
            g_pre = r2(g_pre_mix)
            z_nat = _in_proj(xb, g_pre, w_nat, 1, tm=1024, tn=1408)
            os_, lses = [], []
            for g, (window, dilation) in enumerate(DILATED_GROUPS):
                if g == 0:
                    qkv, col = z_nat, NAT_QKV_COL
                else:
                    qkv, col = _in_proj(xb, g_pre, w_dil[g - 1], dilation, tm=1024, tn=QKV_W), 0
                o_g, lse_g = _attention_group(qkv, col, biases, g)
                os_.append(o_g)
                lses.append(lse_g)
            params = (expand, r2(b_glu), _lane_chunks(w_dw[l]), _lane_chunks(r2(b_dw)), r2(g_conv_ln), r2(b_conv_ln),
                      w_conv_out[l].astype(BF16), r2(b_conv_out), w_attn_out[l].astype(BF16),
                      w_mix_out[l].astype(BF16), r2(g_post_mix))
            x1 = _mixer(xb, z_nat[0], os_, lses, params)
            xb = _ffn(x1, r2(g_pre_ffn), w_ffn_in[l].astype(BF16), w_ffn_out[l].astype(BF16),
                      r2(g_post_ffn))
        outs.append(xb)
    return jnp.stack(outs, axis=0)
```

```python
import functools
import math

import jax
import jax.numpy as jnp
import numpy as np
from jax import lax
from jax.experimental import pallas as pl
from jax.experimental.pallas import tpu as pltpu

D_MODEL = 1024
HEAD_DIM = 64
HEADS_PER_GROUP = 8
DILATED_GROUPS = ((128, 1), (512, 4), (2048, 16))
N_GROUPS = len(DILATED_GROUPS)
GROUP_W = HEADS_PER_GROUP * HEAD_DIM
ATTN_W = N_GROUPS * GROUP_W
QKV_W = 3 * GROUP_W
BLOCK = 128
REL_BUCKETS = 32
REL_MAX_DISTANCE = 2048
CONV_CHANNELS = D_MODEL
CONV_WIDTH = 31
FFN_HIDDEN = 2816
RMS_EPS = 1e-6
LN_EPS = 1e-5
NEG_INF = -1e30
LOG2E = math.log2(math.e)

LANES = 128
SUBLANES = 8
CONV_ROWS = 64
LSE_W = LANES
HALO = 32
VMEM_LIMIT = 56 * 1024 * 1024

BF16 = jnp.bfloat16
F32 = jnp.float32


def _sigmoid(v):
    return 1.0 / (1.0 + jnp.exp(-v))


def _rms(v, g):
    return v * lax.rsqrt(jnp.mean(v * v, axis=-1, keepdims=True) + RMS_EPS) * g


def _dot(a, b):
    return jnp.dot(a, b, preferred_element_type=F32)


def _resident(a):
    return pl.BlockSpec(a.shape, lambda *_: (0,) * a.ndim, pipeline_mode=pl.Buffered(1))


def _causal_dwconv_chunk(k, ext_ref, sh_ref, w_ref, b_ref, out_ref):
    tm = out_ref.shape[0]
    ns = sh_ref.shape[2]
    ext = ext_ref.at[k]
    sh = sh_ref.at[k % 2]
    for s in range(1, SUBLANES):
        sh[s - 1] = ext[s:s + ns, :]
    for r0 in range(0, tm, CONV_ROWS):
        acc = jnp.zeros((CONV_ROWS, LANES), F32) + b_ref[k]
        for j in range(CONV_WIDTH):
            a, s = divmod(HALO - (CONV_WIDTH - 1) + j, SUBLANES)
            src = ext if s == 0 else sh.at[s - 1]
            row = a * SUBLANES + r0
            acc = acc + src[row:row + CONV_ROWS, :] * w_ref[k, j:j + 1, :]
        out_ref[r0:r0 + CONV_ROWS, k * LANES:(k + 1) * LANES] = acc.astype(out_ref.dtype)


def _inproj_kernel(*refs):
    nx = D_MODEL // LANES
    x_refs = refs[:nx]
    (g_ref, w_glu_ref, b_glu_ref, w_gate_ref, w_q0_ref, w_q1_ref, w_q2_ref, w_dw_ref, b_dw_ref,
     cv_ref, gate_ref, q0_ref, q1_ref, q2_ref, h_ref, ext_ref, sh_ref) = refs[nx:]
    tm = x_refs[0].shape[0]
    C = CONV_CHANNELS

    @pl.when(pl.program_id(0) == 0)
    def _():
        ext_ref[:, 0:HALO, :] = jnp.zeros((C // LANES, HALO, LANES), F32)

    g = g_ref[...]
    for gi, (_, r) in enumerate(DILATED_GROUPS):
        n = tm // r
        for c in range(r):
            rows = slice(None) if r == 1 else pl.ds(c, n, stride=r)
            xs = jnp.concatenate([xr[rows, :] for xr in x_refs], axis=-1)
            h_ref[gi, c * n:(c + 1) * n, :] = _rms(xs, g).astype(BF16)
    h = h_ref[0]

    u = ((_dot(h, w_glu_ref[:, :C]) + b_glu_ref[:, :C])
         * _sigmoid(_dot(h, w_glu_ref[:, C:]) + b_glu_ref[:, C:]))
    for k in range(C // LANES):
        ext_ref[k, HALO:HALO + tm, :] = u[:, k * LANES:(k + 1) * LANES]

    for k in range(C // LANES):
        _causal_dwconv_chunk(k, ext_ref, sh_ref, w_dw_ref, b_dw_ref, cv_ref)
    for k in range(C // LANES):
        ext_ref[k, 0:HALO, :] = ext_ref[k, tm:tm + HALO, :]

    gate_ref[...] = _sigmoid(_dot(h, w_gate_ref[...])).astype(BF16)
    for gi, (q_ref, w_ref) in enumerate(((q0_ref, w_q0_ref), (q1_ref, w_q1_ref), (q2_ref, w_q2_ref))):
        q_ref[...] = _dot(h_ref[gi], w_ref[...]).astype(BF16).reshape(q_ref.shape)


def _in_proj(x2, g, w_glu, b_glu, w_gate, w_qkv, w_dw, b_dw, tm=512):
    S = x2.shape[0]
    assert DILATED_GROUPS[0][1] == 1
    nx = D_MODEL // LANES
    x_specs = [pl.BlockSpec((tm, LANES), lambda i, k=k: (i, k)) for k in range(nx)]
    params = (g, w_glu, b_glu, w_gate, *w_qkv, w_dw, b_dw)
    dil = [d for _, d in DILATED_GROUPS]
    return pl.pallas_call(
        _inproj_kernel,
        grid=(S // tm,),
        in_specs=x_specs + [_resident(p) for p in params],
        out_specs=[pl.BlockSpec((tm, CONV_CHANNELS), lambda i: (i, 0)),
                   pl.BlockSpec((tm, 2 * D_MODEL), lambda i: (i, 0))]
                  + [pl.BlockSpec((r, tm // r, QKV_W), lambda i: (0, i, 0)) for r in dil],
        out_shape=[jax.ShapeDtypeStruct((S, CONV_CHANNELS), BF16),
                   jax.ShapeDtypeStruct((S, 2 * D_MODEL), BF16)]
                  + [jax.ShapeDtypeStruct((r, S // r, QKV_W), BF16) for r in dil],
        scratch_shapes=[pltpu.VMEM((N_GROUPS, tm, D_MODEL), BF16),
                        pltpu.VMEM((CONV_CHANNELS // LANES, HALO + tm, LANES), F32),
                        pltpu.VMEM((2, SUBLANES - 1, HALO + tm - SUBLANES, LANES), F32)],
        compiler_params=pltpu.CompilerParams(
            dimension_semantics=("arbitrary",), vmem_limit_bytes=VMEM_LIMIT),
        name="in_proj",
    )(*([x2] * nx), *params)


def _bucket_maps():
    max_exact = REL_BUCKETS // 2
    a = np.arange(BLOCK, dtype=np.int64)[:, None]
    c = np.arange(2 * BLOCK, dtype=np.int64)[None, :]
    offset = a - c + BLOCK
    maps = []
    for window, dilation in DILATED_GROUPS:
        span = window // dilation
        assert span == BLOCK
        d = np.maximum(offset * dilation, 0)
        df = np.maximum(d, 1).astype(np.float32)
        large = max_exact + (np.log(df / np.float32(max_exact)) / np.float32(math.log(REL_MAX_DISTANCE / max_exact))
                             * np.float32(REL_BUCKETS - max_exact)).astype(np.int32)
        large = np.minimum(large, REL_BUCKETS - 1)
        bucket = np.where(d < max_exact, d, large)
        valid = (offset >= 0) & (offset <= span)
        maps.append(np.where(valid, bucket, -1).astype(np.int32))
    return np.stack(maps)


def _bias_kernel(tab_ref, bucket_ref, o_ref):
    g = pl.program_id(0)
    bucket = bucket_ref[...]
    col = lax.broadcasted_iota(jnp.int32, bucket.shape, 1)
    for h in range(HEADS_PER_GROUP):
        acc = jnp.full(bucket.shape, NEG_INF, F32)
        for b in range(REL_BUCKETS):
            acc = jnp.where(bucket == b, tab_ref[b, g * HEADS_PER_GROUP + h] * LOG2E, acc)
        o_ref[0, h] = acc
        o_ref[1, h] = jnp.where(col < BLOCK, NEG_INF, acc)


def _bias_tables(rel_bias_table):
    return pl.pallas_call(
        _bias_kernel,
        grid=(N_GROUPS,),
        in_specs=[
            pl.BlockSpec(memory_space=pltpu.SMEM),
            pl.BlockSpec((None, BLOCK, 2 * BLOCK), lambda g: (g, 0, 0)),
        ],
        out_specs=pl.BlockSpec((None, 2, HEADS_PER_GROUP, BLOCK, 2 * BLOCK), lambda g: (g, 0, 0, 0, 0)),
        out_shape=jax.ShapeDtypeStruct((N_GROUPS, 2, HEADS_PER_GROUP, BLOCK, 2 * BLOCK), F32),
        name="bias_tables",
    )(rel_bias_table.astype(F32), jnp.asarray(_bucket_maps()))


def _attn_kernel(q_ref, kp_ref, kc_ref, vp_ref, vc_ref, bias_ref, sumcol_ref, o_ref, ml_ref):
    first = jnp.where(pl.program_id(1) == 0, 1, 0)
    lane = lax.broadcasted_iota(jnp.int32, (BLOCK, LANES), 1)
    low = lane < HEAD_DIM
    for j in range(q_ref.shape[0] // BLOCK):
        rows = slice(j * BLOCK, (j + 1) * BLOCK)
        variant = first if j == 0 else 0
        ml = jnp.zeros((BLOCK, LANES), F32)
        for pair in range(HEADS_PER_GROUP // 2):
            sl = slice(pair * LANES, (pair + 1) * LANES)
            q = q_ref[rows, sl] * jnp.asarray(HEAD_DIM ** -0.5, BF16)
            if j == 0:
                k = jnp.concatenate([kp_ref[:, sl], kc_ref[rows, sl]], axis=0)
                v = jnp.concatenate([vp_ref[:, sl], vc_ref[rows, sl]], axis=0)
            else:
                k = kc_ref[(j - 1) * BLOCK:(j + 1) * BLOCK, sl]
                v = vc_ref[(j - 1) * BLOCK:(j + 1) * BLOCK, sl]
            halves = []
            for sub in range(2):
                h = 2 * pair + sub
                qm = jnp.where(low if sub == 0 else ~low, q, jnp.zeros_like(q))
                s = lax.dot_general(qm, k, (((1,), (1,)), ((), ())), preferred_element_type=F32)
                t = s * LOG2E + bias_ref[variant, h]
                mt = jnp.max(t, axis=-1, keepdims=True)
                p = jnp.exp2(t - mt).astype(BF16)
                r = _dot(p, jnp.concatenate([v, sumcol_ref[h]], axis=1))
                halves.append(r[:, :LANES])
                ml = jnp.where(lane == h, mt, ml) + r[:, LANES:]
            o_ref[rows, sl] = jnp.where(low, halves[0], halves[1]).astype(o_ref.dtype)
        ml_ref[rows, :] = ml


def _attention_group(qkv, biases, g, nq=4):
    r, L, _ = qkv.shape
    tq = nq * BLOCK
    sumcol = jnp.asarray(np.broadcast_to(
        np.arange(LANES)[None, None, :] == HEADS_PER_GROUP + np.arange(HEADS_PER_GROUP)[:, None, None],
        (HEADS_PER_GROUP, 2 * BLOCK, LANES)), dtype=BF16)

    def spec(part, prev):
        if prev:
            return pl.BlockSpec((None, BLOCK, GROUP_W), lambda c, i: (c, jnp.maximum(i * nq - 1, 0), part))
        return pl.BlockSpec((None, tq, GROUP_W), lambda c, i: (c, i, part))

    return pl.pallas_call(
        _attn_kernel,
        grid=(r, L // tq),
        in_specs=[
            spec(0, False),
            spec(1, True), spec(1, False),
            spec(2, True), spec(2, False),
            pl.BlockSpec((None, 2, HEADS_PER_GROUP, BLOCK, 2 * BLOCK), lambda c, i: (g, 0, 0, 0, 0)),
            pl.BlockSpec(sumcol.shape, lambda c, i: (0, 0, 0)),
        ],
        out_specs=[
            pl.BlockSpec((None, tq, GROUP_W), lambda c, i: (c, i, 0)),
            pl.BlockSpec((None, tq, LSE_W), lambda c, i: (c, i, 0)),
        ],
        out_shape=[
            jax.ShapeDtypeStruct((r, L, GROUP_W), BF16),
            jax.ShapeDtypeStruct((r, L, LSE_W), F32),
        ],
        compiler_params=pltpu.CompilerParams(
            dimension_semantics=("parallel", "arbitrary"), vmem_limit_bytes=VMEM_LIMIT),
        name=f"attn_g{g}",
    )(qkv, qkv, qkv, qkv, qkv, biases, sumcol)


def _mixer_kernel(o0_ref, o1_ref, o2_ref, l0_ref, l1_ref, l2_ref, cv_ref, ga_ref, gc_ref, x_ref,
                  expand_ref, g_ln_ref, b_ln_ref, w_co_ref, b_co_ref, w_ao_ref, w_mo_ref, g_pm_ref,
                  out_ref, on_ref, ln_ref):
    def natural(src_ref, dst_ref):
        r, n, w = src_ref.shape
        if r == 1:
            return src_ref[0].astype(F32)
        for c in range(r):
            for k in range(w // LANES):
                dst_ref[k, pl.ds(c, n, stride=r), :] = src_ref[c, :, k * LANES:(k + 1) * LANES].astype(F32)
        return jnp.concatenate([dst_ref[k] for k in range(w // LANES)], axis=-1)

    ml = [natural(r_, ln_ref.at[g]) for g, r_ in enumerate((l0_ref, l1_ref, l2_ref))]
    den = [pltpu.roll(a, LANES - HEADS_PER_GROUP, axis=1) for a in ml]
    mx = jnp.maximum(jnp.maximum(ml[0], ml[1]), ml[2])
    w = [jnp.exp2(a - mx) for a in ml]
    total = w[0] * den[0] + w[1] * den[1] + w[2] * den[2]
    head_lane = lax.broadcasted_iota(jnp.int32, total.shape, 1) < HEADS_PER_GROUP
    inv = jnp.where(head_lane, 1.0 / total, 0.0)
    expand = expand_ref[...]

    def widen(a):
        hi = a.astype(BF16)
        lo = (a - hi.astype(F32)).astype(BF16)
        return _dot(hi, expand) + _dot(lo, expand)

    o = (widen(w[0] * inv) * natural(o0_ref, on_ref.at[0])
         + widen(w[1] * inv) * natural(o1_ref, on_ref.at[1])
         + widen(w[2] * inv) * natural(o2_ref, on_ref.at[2]))
    y_attn = _dot(o.astype(BF16), w_ao_ref[...])

    acc = cv_ref[...].astype(F32)
    mu = jnp.mean(acc, axis=-1, keepdims=True)
    xc = acc - mu
    y = xc * lax.rsqrt(jnp.mean(xc * xc, axis=-1, keepdims=True) + LN_EPS)
    y = y * g_ln_ref[...] + b_ln_ref[...]
    y = y * _sigmoid(y)
    y_conv = _dot(y.astype(BF16), w_co_ref[...]) + b_co_ref[...]

    merged = ga_ref[...].astype(F32) * y_attn + gc_ref[...].astype(F32) * y_conv
    mix = _dot(merged.astype(BF16), w_mo_ref[...])
    out_ref[...] = x_ref[...] + _rms(mix, g_pm_ref[...])


def _mixer(x2, cv, gates, os_, stats, params, tm=512):
    S = x2.shape[0]

    def row(w, col=0):
        return pl.BlockSpec((tm, w), lambda i: (i, col))

    def rspec(a):
        r, _, w = a.shape
        return pl.BlockSpec((r, tm // r, w), lambda i: (0, i, 0))

    return pl.pallas_call(
        _mixer_kernel,
        grid=(S // tm,),
        in_specs=[rspec(a) for a in os_] + [rspec(a) for a in stats]
                 + [row(CONV_CHANNELS), row(D_MODEL, 0), row(D_MODEL, 1), row(D_MODEL)]
                 + [_resident(p) for p in params],
        out_specs=row(D_MODEL),
        out_shape=jax.ShapeDtypeStruct((S, D_MODEL), F32),
        scratch_shapes=[pltpu.VMEM((N_GROUPS, GROUP_W // LANES, tm, LANES), F32),
                        pltpu.VMEM((N_GROUPS, LSE_W // LANES, tm, LANES), F32)],
        compiler_params=pltpu.CompilerParams(
            dimension_semantics=("parallel",), vmem_limit_bytes=VMEM_LIMIT),
        name="mixer",
    )(*os_, *stats, cv, gates, gates, x2, *params)


def _ffn_kernel(x_ref, g_pre_ref, w_in_ref, w_out_ref, g_post_ref, out_ref, *, chunk):
    x = x_ref[...]
    h = _rms(x, g_pre_ref[...]).astype(BF16)
    acc = jnp.zeros(x.shape, F32)
    for c0 in range(0, FFN_HIDDEN, chunk):
        gate = _dot(h, w_in_ref[:, c0:c0 + chunk])
        up = _dot(h, w_in_ref[:, FFN_HIDDEN + c0:FFN_HIDDEN + c0 + chunk])
        act = (gate * _sigmoid(gate) * up).astype(BF16)
        acc = acc + _dot(act, w_out_ref[c0:c0 + chunk, :])
    out_ref[...] = x + _rms(acc, g_post_ref[...])


def _ffn(x1, g_pre, w_in, w_out, g_post, tm=512, chunk=1408):
    S = x1.shape[0]
    row = pl.BlockSpec((tm, D_MODEL), lambda i: (i, 0))
    params = (g_pre, w_in, w_out, g_post)
    return pl.pallas_call(
        functools.partial(_ffn_kernel, chunk=chunk),
        grid=(S // tm,),
        in_specs=[row] + [_resident(p) for p in params],
        out_specs=row,
        out_shape=jax.ShapeDtypeStruct((S, D_MODEL), F32),
        compiler_params=pltpu.CompilerParams(
            dimension_semantics=("parallel",), vmem_limit_bytes=VMEM_LIMIT),
        name="ffn",
    )(x1, *params)


def _lane_chunks(a):
    return a.reshape(a.shape[0], -1, LANES).transpose(1, 0, 2)


def _split_w_in(w):
    q, k, v, rest = (w[:, :ATTN_W], w[:, ATTN_W:2 * ATTN_W], w[:, 2 * ATTN_W:3 * ATTN_W], w[:, 3 * ATTN_W:])
    w_qkv = []
    for g in range(N_GROUPS):
        sl = slice(g * GROUP_W, (g + 1) * GROUP_W)
        w_qkv.append(jnp.concatenate([q[:, sl], k[:, sl], v[:, sl]], axis=1).astype(BF16))
    w_glu = rest[:, :2 * CONV_CHANNELS].astype(BF16)
    w_gate = rest[:, 2 * CONV_CHANNELS:].astype(BF16)
    return w_glu, w_gate, w_qkv


def kernel(x, rel_bias_table, g_pre_mix, w_in, b_glu, w_dw, b_dw, g_conv_ln, b_conv_ln, w_conv_out, b_conv_out, w_attn_out, w_mix_out, g_post_mix, g_pre_ffn, w_ffn_in, w_ffn_out, g_post_ffn):
    B, S, D = x.shape
    depth = w_in.shape[0]
    expand = jnp.asarray(
        (np.arange(LSE_W)[:, None] == np.arange(GROUP_W)[None, :] // HEAD_DIM), dtype=BF16)
    biases = _bias_tables(rel_bias_table)
    outs = []
    for b in range(B):
        xb = x[b]
        for l in range(depth):
            r2 = lambda a: a[l].reshape(1, -1)
            w_glu, w_gate, w_qkv = _split_w_in(w_in[l])
            cv, gates, *qkvs = _in_proj(xb, r2(g_pre_mix), w_glu, r2(b_glu), w_gate, w_qkv,
                                        _lane_chunks(w_dw[l]), _lane_chunks(r2(b_dw)))
            os_, stats = [], []
            for g, qkv in enumerate(qkvs):
                o_g, ml_g = _attention_group(qkv, biases, g)
                os_.append(o_g)
                stats.append(ml_g)
            params = (expand, r2(g_conv_ln), r2(b_conv_ln), w_conv_out[l].astype(BF16), r2(b_conv_out),
                      w_attn_out[l].astype(BF16), w_mix_out[l].astype(BF16), r2(g_post_mix))
            x1 = _mixer(xb, cv, gates, os_, stats, params)
            xb = _ffn(x1, r2(g_pre_ffn), w_ffn_in[l].astype(BF16), w_ffn_out[l].astype(BF16),
                      r2(g_post_ffn))
        outs.append(xb)
    return jnp.stack(outs, axis=0)
```

```python
import functools
import math

import jax
import jax.numpy as jnp
import numpy as np
from jax import lax
from jax.experimental import pallas as pl
from jax.experimental.pallas import tpu as pltpu

D_MODEL = 1024
HEAD_DIM = 64
HEADS_PER_GROUP = 8
DILATED_GROUPS = ((128, 1), (512, 4), (2048, 16))
N_GROUPS = len(DILATED_GROUPS)
GROUP_W = HEADS_PER_GROUP * HEAD_DIM
ATTN_W = N_GROUPS * GROUP_W
QKV_W = 3 * GROUP_W
BLOCK = 128
REL_BUCKETS = 32
REL_MAX_DISTANCE = 2048
CONV_CHANNELS = D_MODEL
CONV_WIDTH = 31
FFN_HIDDEN = 2816
RMS_EPS = 1e-6
LN_EPS = 1e-5
NEG_INF = -1e30
LOG2E = math.log2(math.e)

LANES = 128
SUBLANES = 8
CONV_ROWS = 64
LSE_W = LANES
HALO = 32
VMEM_LIMIT = 56 * 1024 * 1024

BF16 = jnp.bfloat16
F32 = jnp.float32


def _sigmoid(v):
    return 0.5 * jnp.tanh(0.5 * v) + 0.5


def _rms(v, g):
    return v * lax.rsqrt(jnp.mean(v * v, axis=-1, keepdims=True) + RMS_EPS) * g


def _dot(a, b):
    return jnp.dot(a, b, preferred_element_type=F32)


def _resident(a):
    return pl.BlockSpec(a.shape, lambda *_: (0,) * a.ndim, pipeline_mode=pl.Buffered(1))


def _causal_dwconv_chunk(k, ext_ref, sh_ref, w_ref, b_ref, out_ref):
    tm = out_ref.shape[0]
    ns = sh_ref.shape[1]
    ext = ext_ref.at[k]
    lanes = slice(k * LANES, (k + 1) * LANES)
    for s in range(1, SUBLANES):
        sh_ref[s - 1] = ext[s:s + ns, :]
    for r0 in range(0, tm, CONV_ROWS):
        acc = jnp.zeros((CONV_ROWS, LANES), F32) + b_ref[k]
        for j in range(CONV_WIDTH):
            a, s = divmod(HALO - (CONV_WIDTH - 1) + j, SUBLANES)
            src = ext if s == 0 else sh_ref.at[s - 1]
            row = a * SUBLANES + r0
            acc = acc + src[row:row + CONV_ROWS, :] * w_ref[k, j:j + 1, :]
        out_ref[r0:r0 + CONV_ROWS, lanes] = acc.astype(out_ref.dtype)


def _inproj_kernel(x_ref, g_ref, w_glu_ref, b_glu_ref, w_gate_ref, w_q0_ref, w_q1_ref, w_q2_ref,
                   w_dw_ref, b_dw_ref, cv_ref, gate_ref, q0_ref, q1_ref, q2_ref,
                   h_ref, xn_ref, ext_ref, sh_ref):
    nx = D_MODEL // LANES
    tm = x_ref.shape[0]
    C = CONV_CHANNELS

    @pl.when(pl.program_id(0) == 0)
    def _():
        ext_ref[:, 0:HALO, :] = jnp.zeros((C // LANES, HALO, LANES), F32)

    xn = _rms(x_ref[...], g_ref[...])
    h_ref[0] = xn.astype(BF16)
    for k in range(nx):
        xn_ref[k] = xn[:, k * LANES:(k + 1) * LANES]
    for gi, (_, r) in enumerate(DILATED_GROUPS):
        if r == 1:
            continue
        n = tm // r
        for c in range(r):
            rows = jnp.concatenate([xn_ref[k, pl.ds(c, n, stride=r), :] for k in range(nx)], axis=-1)
            h_ref[gi, c * n:(c + 1) * n, :] = rows.astype(BF16)

    u = ((_dot(h_ref[0], w_glu_ref[:, :C]) + b_glu_ref[:, :C])
         * _sigmoid(_dot(h_ref[0], w_glu_ref[:, C:]) + b_glu_ref[:, C:]))
    for k in range(C // LANES):
        ext_ref[k, HALO:HALO + tm, :] = u[:, k * LANES:(k + 1) * LANES]

    for k in range(C // LANES):
        _causal_dwconv_chunk(k, ext_ref, sh_ref.at[k % 2], w_dw_ref, b_dw_ref, cv_ref)
    for k in range(C // LANES):
        ext_ref[k, 0:HALO, :] = ext_ref[k, tm:tm + HALO, :]

    gate_ref[...] = _sigmoid(_dot(h_ref[0], w_gate_ref[...])).astype(BF16)
    for gi, (q_ref, w_ref) in enumerate(((q0_ref, w_q0_ref), (q1_ref, w_q1_ref), (q2_ref, w_q2_ref))):
        q_ref[...] = _dot(h_ref[gi], w_ref[...]).astype(BF16).reshape(q_ref.shape)


def _in_proj(x2, g, w_glu, b_glu, w_gate, w_qkv, w_dw, b_dw, tm=512):
    S = x2.shape[0]
    assert DILATED_GROUPS[0][1] == 1
    params = (g, w_glu, b_glu, w_gate, *w_qkv, w_dw, b_dw)
    dil = [d for _, d in DILATED_GROUPS]
    return pl.pallas_call(
        _inproj_kernel,
        grid=(S // tm,),
        in_specs=[pl.BlockSpec((tm, D_MODEL), lambda i: (i, 0))] + [_resident(p) for p in params],
        out_specs=[pl.BlockSpec((tm, CONV_CHANNELS), lambda i: (i, 0)),
                   pl.BlockSpec((tm, 2 * D_MODEL), lambda i: (i, 0))]
                  + [pl.BlockSpec((r, tm // r, QKV_W), lambda i: (0, i, 0)) for r in dil],
        out_shape=[jax.ShapeDtypeStruct((S, CONV_CHANNELS), BF16),
                   jax.ShapeDtypeStruct((S, 2 * D_MODEL), BF16)]
                  + [jax.ShapeDtypeStruct((r, S // r, QKV_W), BF16) for r in dil],
        scratch_shapes=[pltpu.VMEM((N_GROUPS, tm, D_MODEL), BF16),
                        pltpu.VMEM((D_MODEL // LANES, tm, LANES), F32),
                        pltpu.VMEM((CONV_CHANNELS // LANES, HALO + tm, LANES), F32),
                        pltpu.VMEM((2, SUBLANES - 1, HALO + tm - SUBLANES, LANES), F32)],
        compiler_params=pltpu.CompilerParams(
            dimension_semantics=("arbitrary",), vmem_limit_bytes=VMEM_LIMIT),
        name="in_proj",
    )(x2, *params)


def _bucket_maps():
    max_exact = REL_BUCKETS // 2
    a = np.arange(BLOCK, dtype=np.int64)[:, None]
    c = np.arange(2 * BLOCK, dtype=np.int64)[None, :]
    offset = a - c + BLOCK
    maps = []
    for window, dilation in DILATED_GROUPS:
        span = window // dilation
        assert span == BLOCK
        d = np.maximum(offset * dilation, 0)
        df = np.maximum(d, 1).astype(np.float32)
        large = max_exact + (np.log(df / np.float32(max_exact)) / np.float32(math.log(REL_MAX_DISTANCE / max_exact))
                             * np.float32(REL_BUCKETS - max_exact)).astype(np.int32)
        large = np.minimum(large, REL_BUCKETS - 1)
        bucket = np.where(d < max_exact, d, large)
        valid = (offset >= 0) & (offset <= span)
        maps.append(np.where(valid, bucket, -1).astype(np.int32))
    return np.stack(maps)


def _bias_kernel(tab_ref, bucket_ref, o_ref):
    g = pl.program_id(0)
    bucket = bucket_ref[...]
    col = lax.broadcasted_iota(jnp.int32, bucket.shape, 1)
    in_bucket = [bucket == b for b in range(REL_BUCKETS)]
    for h in range(HEADS_PER_GROUP):
        acc = jnp.full(bucket.shape, NEG_INF, F32)
        for b in range(REL_BUCKETS):
            acc = jnp.where(in_bucket[b], tab_ref[b, g * HEADS_PER_GROUP + h] * LOG2E, acc)
        o_ref[0, h] = acc
        o_ref[1, h] = jnp.where(col < BLOCK, NEG_INF, acc)


def _bias_tables(rel_bias_table):
    return pl.pallas_call(
        _bias_kernel,
        grid=(N_GROUPS,),
        in_specs=[
            pl.BlockSpec(memory_space=pltpu.SMEM),
            pl.BlockSpec((None, BLOCK, 2 * BLOCK), lambda g: (g, 0, 0)),
        ],
        out_specs=pl.BlockSpec((None, 2, HEADS_PER_GROUP, BLOCK, 2 * BLOCK), lambda g: (g, 0, 0, 0, 0)),
        out_shape=jax.ShapeDtypeStruct((N_GROUPS, 2, HEADS_PER_GROUP, BLOCK, 2 * BLOCK), F32),
        name="bias_tables",
    )(rel_bias_table.astype(F32), jnp.asarray(_bucket_maps()))


def _attn_kernel(q_ref, kp_ref, kc_ref, vp_ref, vc_ref, bias_ref, sumcol_ref, o_ref, ml_ref):
    first = jnp.where(pl.program_id(1) == 0, 1, 0)
    lane = lax.broadcasted_iota(jnp.int32, (BLOCK, LANES), 1)
    low = lane < HEAD_DIM
    for j in range(q_ref.shape[0] // BLOCK):
        rows = slice(j * BLOCK, (j + 1) * BLOCK)
        variant = first if j == 0 else 0
        ml = jnp.zeros((BLOCK, LANES), F32)
        for pair in range(HEADS_PER_GROUP // 2):
            sl = slice(pair * LANES, (pair + 1) * LANES)
            q = q_ref[rows, sl] * jnp.asarray(HEAD_DIM ** -0.5, BF16)
            if j == 0:
                k = jnp.concatenate([kp_ref[:, sl], kc_ref[rows, sl]], axis=0)
                v = jnp.concatenate([vp_ref[:, sl], vc_ref[rows, sl]], axis=0)
            else:
                k = kc_ref[(j - 1) * BLOCK:(j + 1) * BLOCK, sl]
                v = vc_ref[(j - 1) * BLOCK:(j + 1) * BLOCK, sl]
            halves = []
            for sub in range(2):
                h = 2 * pair + sub
                qm = jnp.where(low if sub == 0 else ~low, q, jnp.zeros_like(q))
                s = lax.dot_general(qm, k, (((1,), (1,)), ((), ())), preferred_element_type=F32)
                t = s * LOG2E + bias_ref[variant, h]
                mt = jnp.max(t, axis=-1, keepdims=True)
                p = jnp.exp2(t - mt).astype(BF16)
                r = _dot(p, jnp.concatenate([v, sumcol_ref[h]], axis=1))
                halves.append(r[:, :LANES])
                ml = jnp.where(lane == h, mt, ml) + r[:, LANES:]
            o_ref[rows, sl] = jnp.where(low, halves[0], halves[1]).astype(o_ref.dtype)
        ml_ref[rows, :] = ml


def _attention_group(qkv, biases, g, nq=8):
    r, L, _ = qkv.shape
    tq = nq * BLOCK
    sumcol = jnp.asarray(np.broadcast_to(
        np.arange(LANES)[None, None, :] == HEADS_PER_GROUP + np.arange(HEADS_PER_GROUP)[:, None, None],
        (HEADS_PER_GROUP, 2 * BLOCK, LANES)), dtype=BF16)

    def spec(part, prev):
        if prev:
            return pl.BlockSpec((None, BLOCK, GROUP_W), lambda c, i: (c, jnp.maximum(i * nq - 1, 0), part))
        return pl.BlockSpec((None, tq, GROUP_W), lambda c, i: (c, i, part))

    return pl.pallas_call(
        _attn_kernel,
        grid=(r, L // tq),
        in_specs=[
            spec(0, False),
            spec(1, True), spec(1, False),
            spec(2, True), spec(2, False),
            pl.BlockSpec((None, 2, HEADS_PER_GROUP, BLOCK, 2 * BLOCK), lambda c, i: (g, 0, 0, 0, 0)),
            pl.BlockSpec(sumcol.shape, lambda c, i: (0, 0, 0)),
        ],
        out_specs=[
            pl.BlockSpec((None, tq, GROUP_W), lambda c, i: (c, i, 0)),
            pl.BlockSpec((None, tq, LSE_W), lambda c, i: (c, i, 0)),
        ],
        out_shape=[
            jax.ShapeDtypeStruct((r, L, GROUP_W), BF16),
            jax.ShapeDtypeStruct((r, L, LSE_W), F32),
        ],
        compiler_params=pltpu.CompilerParams(
            dimension_semantics=("parallel", "arbitrary"), vmem_limit_bytes=VMEM_LIMIT),
        name=f"attn_g{g}",
    )(qkv, qkv, qkv, qkv, qkv, biases, sumcol)


def _mixer_kernel(o0_ref, o1_ref, o2_ref, l0_ref, l1_ref, l2_ref, cv_ref, ga_ref, gc_ref, x_ref,
                  expand_ref, g_ln_ref, b_ln_ref, w_co_ref, b_co_ref, w_ao_ref, w_mo_ref, g_pm_ref,
                  out_ref, on_ref, ln_ref):
    def natural(src_ref, dst_ref):
        r, n, w = src_ref.shape
        if r == 1:
            return src_ref[0].astype(F32)
        for c in range(r):
            for k in range(w // LANES):
                dst_ref[k, pl.ds(c, n, stride=r), :] = src_ref[c, :, k * LANES:(k + 1) * LANES].astype(F32)
        return jnp.concatenate([dst_ref[k] for k in range(w // LANES)], axis=-1)

    ml = [natural(r_, ln_ref.at[g]) for g, r_ in enumerate((l0_ref, l1_ref, l2_ref))]
    den = [pltpu.roll(a, LANES - HEADS_PER_GROUP, axis=1) for a in ml]
    mx = jnp.maximum(jnp.maximum(ml[0], ml[1]), ml[2])
    w = [jnp.exp2(a - mx) for a in ml]
    total = w[0] * den[0] + w[1] * den[1] + w[2] * den[2]
    head_lane = lax.broadcasted_iota(jnp.int32, total.shape, 1) < HEADS_PER_GROUP
    inv = jnp.where(head_lane, 1.0 / total, 0.0)
    expand = expand_ref[...]

    def widen(a):
        hi = a.astype(BF16)
        lo = (a - hi.astype(F32)).astype(BF16)
        return _dot(hi, expand) + _dot(lo, expand)

    o = (widen(w[0] * inv) * natural(o0_ref, on_ref.at[0])
         + widen(w[1] * inv) * natural(o1_ref, on_ref.at[1])
         + widen(w[2] * inv) * natural(o2_ref, on_ref.at[2]))
    y_attn = _dot(o.astype(BF16), w_ao_ref[...])

    acc = cv_ref[...].astype(F32)
    mu = jnp.mean(acc, axis=-1, keepdims=True)
    xc = acc - mu
    y = xc * lax.rsqrt(jnp.mean(xc * xc, axis=-1, keepdims=True) + LN_EPS)
    y = y * g_ln_ref[...] + b_ln_ref[...]
    y = y * _sigmoid(y)
    y_conv = _dot(y.astype(BF16), w_co_ref[...]) + b_co_ref[...]

    merged = ga_ref[...].astype(F32) * y_attn + gc_ref[...].astype(F32) * y_conv
    mix = _dot(merged.astype(BF16), w_mo_ref[...])
    out_ref[...] = x_ref[...] + _rms(mix, g_pm_ref[...])


def _mixer(x2, cv, gates, os_, stats, params, tm=512):
    S = x2.shape[0]

    def row(w, col=0):
        return pl.BlockSpec((tm, w), lambda i: (i, col))

    def rspec(a):
        r, _, w = a.shape
        return pl.BlockSpec((r, tm // r, w), lambda i: (0, i, 0))

    return pl.pallas_call(
        _mixer_kernel,
        grid=(S // tm,),
        in_specs=[rspec(a) for a in os_] + [rspec(a) for a in stats]
                 + [row(CONV_CHANNELS), row(D_MODEL, 0), row(D_MODEL, 1), row(D_MODEL)]
                 + [_resident(p) for p in params],
        out_specs=row(D_MODEL),
        out_shape=jax.ShapeDtypeStruct((S, D_MODEL), F32),
        scratch_shapes=[pltpu.VMEM((N_GROUPS, GROUP_W // LANES, tm, LANES), F32),
                        pltpu.VMEM((N_GROUPS, LSE_W // LANES, tm, LANES), F32)],
        compiler_params=pltpu.CompilerParams(
            dimension_semantics=("parallel",), vmem_limit_bytes=VMEM_LIMIT),
        name="mixer",
    )(*os_, *stats, cv, gates, gates, x2, *params)


def _ffn_kernel(x_ref, g_pre_ref, w_in_ref, w_out_ref, g_post_ref, out_ref, *, chunk):
    x = x_ref[...]
    h = _rms(x, g_pre_ref[...]).astype(BF16)
    acc = jnp.zeros(x.shape, F32)
    for c0 in range(0, FFN_HIDDEN, chunk):
        gate = _dot(h, w_in_ref[:, c0:c0 + chunk])
        up = _dot(h, w_in_ref[:, FFN_HIDDEN + c0:FFN_HIDDEN + c0 + chunk])
        act = (gate * _sigmoid(gate) * up).astype(BF16)
        acc = acc + _dot(act, w_out_ref[c0:c0 + chunk, :])
    out_ref[...] = x + _rms(acc, g_post_ref[...])


def _ffn(x1, g_pre, w_in, w_out, g_post, tm=512, chunk=1408):
    S = x1.shape[0]
    row = pl.BlockSpec((tm, D_MODEL), lambda i: (i, 0))
    params = (g_pre, w_in, w_out, g_post)
    return pl.pallas_call(
        functools.partial(_ffn_kernel, chunk=chunk),
        grid=(S // tm,),
        in_specs=[row] + [_resident(p) for p in params],
        out_specs=row,
        out_shape=jax.ShapeDtypeStruct((S, D_MODEL), F32),
        compiler_params=pltpu.CompilerParams(
            dimension_semantics=("parallel",), vmem_limit_bytes=VMEM_LIMIT),
        name="ffn",
    )(x1, *params)


def _lane_chunks(a):
    return a.reshape(a.shape[0], -1, LANES).transpose(1, 0, 2)


def _split_w_in(w):
    q, k, v, rest = (w[:, :ATTN_W], w[:, ATTN_W:2 * ATTN_W], w[:, 2 * ATTN_W:3 * ATTN_W], w[:, 3 * ATTN_W:])
    w_qkv = []
    for g in range(N_GROUPS):
        sl = slice(g * GROUP_W, (g + 1) * GROUP_W)
        w_qkv.append(jnp.concatenate([q[:, sl], k[:, sl], v[:, sl]], axis=1).astype(BF16))
    w_glu = rest[:, :2 * CONV_CHANNELS].astype(BF16)
    w_gate = rest[:, 2 * CONV_CHANNELS:].astype(BF16)
    return w_glu, w_gate, w_qkv


def kernel(x, rel_bias_table, g_pre_mix, w_in, b_glu, w_dw, b_dw, g_conv_ln, b_conv_ln, w_conv_out, b_conv_out, w_attn_out, w_mix_out, g_post_mix, g_pre_ffn, w_ffn_in, w_ffn_out, g_post_ffn):
    B, S, D = x.shape
    depth = w_in.shape[0]
    expand = jnp.asarray(
        (np.arange(LSE_W)[:, None] == np.arange(GROUP_W)[None, :] // HEAD_DIM), dtype=BF16)
    biases = _bias_tables(rel_bias_table)
    outs = []
    for b in range(B):
        xb = x[b]
        for l in range(depth):
            r2 = lambda a: a[l].reshape(1, -1)
            w_glu, w_gate, w_qkv = _split_w_in(w_in[l])
            cv, gates, *qkvs = _in_proj(xb, r2(g_pre_mix), w_glu, r2(b_glu), w_gate, w_qkv,
                                        _lane_chunks(w_dw[l]), _lane_chunks(r2(b_dw)))
            os_, stats = [], []
            for g, qkv in enumerate(qkvs):
                o_g, ml_g = _attention_group(qkv, biases, g)
                os_.append(o_g)
                stats.append(ml_g)
            params = (expand, r2(g_conv_ln), r2(b_conv_ln), w_conv_out[l].astype(BF16), r2(b_conv_out),
                      w_attn_out[l].astype(BF16), w_mix_out[l].astype(BF16), r2(g_post_mix))
            x1 = _mixer(xb, cv, gates, os_, stats, params)
            xb = _ffn(x1, r2(g_pre_ffn), w_ffn_in[l].astype(BF16), w_ffn_out[l].astype(BF16),
                      r2(g_post_ffn))
        outs.append(xb)
    return jnp.stack(outs, axis=0)
```

```python
import functools
import math

import jax
import jax.numpy as jnp
import numpy as np
from jax import lax
from jax.experimental import pallas as pl
from jax.experimental.pallas import tpu as pltpu

D_MODEL = 1024
HEAD_DIM = 64
HEADS_PER_GROUP = 8
DILATED_GROUPS = ((128, 1), (512, 4), (2048, 16))
N_GROUPS = len(DILATED_GROUPS)
GROUP_W = HEADS_PER_GROUP * HEAD_DIM
ATTN_W = N_GROUPS * GROUP_W
QKV_W = 3 * GROUP_W
BLOCK = 128
REL_BUCKETS = 32
REL_MAX_DISTANCE = 2048
CONV_CHANNELS = D_MODEL
CONV_WIDTH = 31
FFN_HIDDEN = 2816
RMS_EPS = 1e-6
LN_EPS = 1e-5
NEG_INF = -1e30
LOG2E = math.log2(math.e)

LANES = 128
SUBLANES = 8
CONV_ROWS = 64
LSE_W = LANES
HALO = 32
VMEM_LIMIT = 56 * 1024 * 1024

BF16 = jnp.bfloat16
F32 = jnp.float32


def _sigmoid_of_twice(h):
    return 0.5 * jnp.tanh(h) + 0.5


def _silu_of_twice(h):
    return h * (jnp.tanh(h) + 1.0)


def _rms(v, g):
    return v * lax.rsqrt(jnp.mean(v * v, axis=-1, keepdims=True) + RMS_EPS) * g


def _dot(a, b):
    return jnp.dot(a, b, preferred_element_type=F32)


def _resident(a):
    return pl.BlockSpec(a.shape, lambda *_: (0,) * a.ndim, pipeline_mode=pl.Buffered(1))


def _causal_dwconv_chunk(k, ext_ref, sh_ref, w_ref, bias, out_ref):
    tm = out_ref.shape[0]
    ns = sh_ref.shape[1]
    ext = ext_ref.at[k]
    lanes = slice(k * LANES, (k + 1) * LANES)
    for s in range(1, SUBLANES):
        sh_ref[s - 1] = ext[s:s + ns, :]
    for r0 in range(0, tm, CONV_ROWS):
        acc = jnp.zeros((CONV_ROWS, LANES), F32) + bias
        for j in range(CONV_WIDTH):
            a, s = divmod(HALO - (CONV_WIDTH - 1) + j, SUBLANES)
            src = ext if s == 0 else sh_ref.at[s - 1]
            row = a * SUBLANES + r0
            acc = acc + src[row:row + CONV_ROWS, :] * w_ref[k, j:j + 1, :]
        out_ref[r0:r0 + CONV_ROWS, lanes] = acc.astype(out_ref.dtype)


def _inproj_kernel(x_ref, g_ref, w_glu_ref, b_glu_ref, w_gate_ref, w_q0_ref, w_q1_ref, w_q2_ref,
                   w_dw_ref, b_dw_ref, cv_ref, gate_ref, q0_ref, q1_ref, q2_ref,
                   h_ref, xn_ref, ext_ref, sh_ref):
    nx = D_MODEL // LANES
    tm = x_ref.shape[0]
    C = CONV_CHANNELS

    @pl.when(pl.program_id(0) == 0)
    def _():
        ext_ref[:, 0:HALO, :] = jnp.zeros((C // LANES, HALO, LANES), F32)

    xn = _rms(x_ref[...], g_ref[...])
    h_ref[0] = xn.astype(BF16)
    for k in range(nx):
        xn_ref[k] = xn[:, k * LANES:(k + 1) * LANES]
    for gi, (_, r) in enumerate(DILATED_GROUPS):
        if r == 1:
            continue
        n = tm // r
        for c in range(r):
            rows = jnp.concatenate([xn_ref[k, pl.ds(c, n, stride=r), :] for k in range(nx)], axis=-1)
            h_ref[gi, c * n:(c + 1) * n, :] = rows.astype(BF16)

    for k in range(C // LANES):
        cols = slice(2 * k * LANES, 2 * (k + 1) * LANES)
        z = _dot(h_ref[0], w_glu_ref[:, cols]) + b_glu_ref[:, cols]
        ext_ref[k, HALO:HALO + tm, :] = z[:, :LANES] * (jnp.tanh(z[:, LANES:]) + 1.0)
        _causal_dwconv_chunk(k, ext_ref, sh_ref.at[k % 2], w_dw_ref, b_dw_ref[k], cv_ref)
    for k in range(C // LANES):
        ext_ref[k, 0:HALO, :] = ext_ref[k, tm:tm + HALO, :]

    gate_ref[...] = _sigmoid_of_twice(_dot(h_ref[0], w_gate_ref[...])).astype(BF16)
    for gi, (q_ref, w_ref) in enumerate(((q0_ref, w_q0_ref), (q1_ref, w_q1_ref), (q2_ref, w_q2_ref))):
        q_ref[...] = _dot(h_ref[gi], w_ref[...]).astype(BF16).reshape(q_ref.shape)


def _in_proj(x2, g, w_glu, b_glu, w_gate, w_qkv, w_dw, b_dw, tm=512):
    S = x2.shape[0]
    assert DILATED_GROUPS[0][1] == 1
    params = (g, w_glu, b_glu, w_gate, *w_qkv, w_dw, b_dw)
    dil = [d for _, d in DILATED_GROUPS]
    return pl.pallas_call(
        _inproj_kernel,
        grid=(S // tm,),
        in_specs=[pl.BlockSpec((tm, D_MODEL), lambda i: (i, 0))] + [_resident(p) for p in params],
        out_specs=[pl.BlockSpec((tm, CONV_CHANNELS), lambda i: (i, 0)),
                   pl.BlockSpec((tm, 2 * D_MODEL), lambda i: (i, 0))]
                  + [pl.BlockSpec((r, tm // r, QKV_W), lambda i: (0, i, 0)) for r in dil],
        out_shape=[jax.ShapeDtypeStruct((S, CONV_CHANNELS), BF16),
                   jax.ShapeDtypeStruct((S, 2 * D_MODEL), BF16)]
                  + [jax.ShapeDtypeStruct((r, S // r, QKV_W), BF16) for r in dil],
        scratch_shapes=[pltpu.VMEM((N_GROUPS, tm, D_MODEL), BF16),
                        pltpu.VMEM((D_MODEL // LANES, tm, LANES), F32),
                        pltpu.VMEM((CONV_CHANNELS // LANES, HALO + tm, LANES), F32),
                        pltpu.VMEM((2, SUBLANES - 1, HALO + tm - SUBLANES, LANES), F32)],
        compiler_params=pltpu.CompilerParams(
            dimension_semantics=("arbitrary",), vmem_limit_bytes=VMEM_LIMIT),
        name="in_proj",
    )(x2, *params)


def _bucket_maps():
    max_exact = REL_BUCKETS // 2
    a = np.arange(BLOCK, dtype=np.int64)[:, None]
    c = np.arange(2 * BLOCK, dtype=np.int64)[None, :]
    offset = a - c + BLOCK
    maps = []
    for window, dilation in DILATED_GROUPS:
        span = window // dilation
        assert span == BLOCK
        d = np.maximum(offset * dilation, 0)
        df = np.maximum(d, 1).astype(np.float32)
        large = max_exact + (np.log(df / np.float32(max_exact)) / np.float32(math.log(REL_MAX_DISTANCE / max_exact))
                             * np.float32(REL_BUCKETS - max_exact)).astype(np.int32)
        large = np.minimum(large, REL_BUCKETS - 1)
        bucket = np.where(d < max_exact, d, large)
        valid = (offset >= 0) & (offset <= span)
        maps.append(np.where(valid, bucket, -1).astype(np.int32))
    return np.stack(maps)


def _bias_kernel(tab_ref, bucket_ref, o_ref):
    g = pl.program_id(0)
    bucket = bucket_ref[...]
    col = lax.broadcasted_iota(jnp.int32, bucket.shape, 1)
    in_bucket = [bucket == b for b in range(REL_BUCKETS)]
    for h in range(HEADS_PER_GROUP):
        acc = jnp.full(bucket.shape, NEG_INF, F32)
        for b in range(REL_BUCKETS):
            acc = jnp.where(in_bucket[b], tab_ref[b, g * HEADS_PER_GROUP + h] * LOG2E, acc)
        o_ref[0, h] = acc
        o_ref[1, h] = jnp.where(col < BLOCK, NEG_INF, acc)


def _bias_tables(rel_bias_table):
    return pl.pallas_call(
        _bias_kernel,
        grid=(N_GROUPS,),
        in_specs=[
            pl.BlockSpec(memory_space=pltpu.SMEM),
            pl.BlockSpec((None, BLOCK, 2 * BLOCK), lambda g: (g, 0, 0)),
        ],
        out_specs=pl.BlockSpec((None, 2, HEADS_PER_GROUP, BLOCK, 2 * BLOCK), lambda g: (g, 0, 0, 0, 0)),
        out_shape=jax.ShapeDtypeStruct((N_GROUPS, 2, HEADS_PER_GROUP, BLOCK, 2 * BLOCK), F32),
        name="bias_tables",
    )(rel_bias_table.astype(F32), jnp.asarray(_bucket_maps()))


def _attn_kernel(q_ref, kp_ref, kc_ref, vp_ref, vc_ref, bias_ref, sumcol_ref, o_ref, ml_ref):
    first = jnp.where(pl.program_id(1) == 0, 1, 0)
    lane = lax.broadcasted_iota(jnp.int32, (BLOCK, LANES), 1)
    low = lane < HEAD_DIM
    for j in range(q_ref.shape[0] // BLOCK):
        rows = slice(j * BLOCK, (j + 1) * BLOCK)
        variant = first if j == 0 else 0
        ml = jnp.zeros((BLOCK, LANES), F32)
        for pair in range(HEADS_PER_GROUP // 2):
            sl = slice(pair * LANES, (pair + 1) * LANES)
            q = q_ref[rows, sl] * jnp.asarray(HEAD_DIM ** -0.5, BF16)
            if j == 0:
                k = jnp.concatenate([kp_ref[:, sl], kc_ref[rows, sl]], axis=0)
                v = jnp.concatenate([vp_ref[:, sl], vc_ref[rows, sl]], axis=0)
            else:
                k = kc_ref[(j - 1) * BLOCK:(j + 1) * BLOCK, sl]
                v = vc_ref[(j - 1) * BLOCK:(j + 1) * BLOCK, sl]
            halves = []
            for sub in range(2):
                h = 2 * pair + sub
                qm = jnp.where(low if sub == 0 else ~low, q, jnp.zeros_like(q))
                s = lax.dot_general(qm, k, (((1,), (1,)), ((), ())), preferred_element_type=F32)
                t = s * LOG2E + bias_ref[variant, h]
                mt = jnp.max(t, axis=-1, keepdims=True)
                p = jnp.exp2(t - mt).astype(BF16)
                r = _dot(p, jnp.concatenate([v, sumcol_ref[h]], axis=1))
                halves.append(r[:, :LANES])
                ml = jnp.where(lane == h, mt, ml) + r[:, LANES:]
            o_ref[rows, sl] = jnp.where(low, halves[0], halves[1]).astype(o_ref.dtype)
        ml_ref[rows, :] = ml


def _attention_group(qkv, biases, g, nq=8):
    r, L, _ = qkv.shape
    tq = nq * BLOCK
    sumcol = jnp.asarray(np.broadcast_to(
        np.arange(LANES)[None, None, :] == HEADS_PER_GROUP + np.arange(HEADS_PER_GROUP)[:, None, None],
        (HEADS_PER_GROUP, 2 * BLOCK, LANES)), dtype=BF16)

    def spec(part, prev):
        if prev:
            return pl.BlockSpec((None, BLOCK, GROUP_W), lambda c, i: (c, jnp.maximum(i * nq - 1, 0), part))
        return pl.BlockSpec((None, tq, GROUP_W), lambda c, i: (c, i, part))

    return pl.pallas_call(
        _attn_kernel,
        grid=(r, L // tq),
        in_specs=[
            spec(0, False),
            spec(1, True), spec(1, False),
            spec(2, True), spec(2, False),
            pl.BlockSpec((None, 2, HEADS_PER_GROUP, BLOCK, 2 * BLOCK), lambda c, i: (g, 0, 0, 0, 0)),
            pl.BlockSpec(sumcol.shape, lambda c, i: (0, 0, 0)),
        ],
        out_specs=[
            pl.BlockSpec((None, tq, GROUP_W), lambda c, i: (c, i, 0)),
            pl.BlockSpec((None, tq, LSE_W), lambda c, i: (c, i, 0)),
        ],
        out_shape=[
            jax.ShapeDtypeStruct((r, L, GROUP_W), BF16),
            jax.ShapeDtypeStruct((r, L, LSE_W), F32),
        ],
        compiler_params=pltpu.CompilerParams(
            dimension_semantics=("parallel", "arbitrary"), vmem_limit_bytes=VMEM_LIMIT),
        name=f"attn_g{g}",
    )(qkv, qkv, qkv, qkv, qkv, biases, sumcol)


def _mixer_kernel(o0_ref, o1_ref, o2_ref, l0_ref, l1_ref, l2_ref, cv_ref, ga_ref, gc_ref, x_ref,
                  expand_ref, g_ln_ref, b_ln_ref, w_co_ref, b_co_ref, w_ao_ref, w_mo_ref, g_pm_ref,
                  out_ref, on_ref, ln_ref):
    def natural(src_ref, dst_ref):
        r, n, w = src_ref.shape
        if r == 1:
            return src_ref[0].astype(F32)
        for c in range(r):
            for k in range(w // LANES):
                dst_ref[k, pl.ds(c, n, stride=r), :] = src_ref[c, :, k * LANES:(k + 1) * LANES].astype(F32)
        return jnp.concatenate([dst_ref[k] for k in range(w // LANES)], axis=-1)

    ml = [natural(r_, ln_ref.at[g]) for g, r_ in enumerate((l0_ref, l1_ref, l2_ref))]
    den = [pltpu.roll(a, LANES - HEADS_PER_GROUP, axis=1) for a in ml]
    mx = jnp.maximum(jnp.maximum(ml[0], ml[1]), ml[2])
    w = [jnp.exp2(a - mx) for a in ml]
    total = w[0] * den[0] + w[1] * den[1] + w[2] * den[2]
    head_lane = lax.broadcasted_iota(jnp.int32, total.shape, 1) < HEADS_PER_GROUP
    inv = jnp.where(head_lane, 1.0 / total, 0.0)
    expand = expand_ref[...]

    def widen(a):
        hi = a.astype(BF16)
        lo = (a - hi.astype(F32)).astype(BF16)
        return _dot(hi, expand) + _dot(lo, expand)

    o = (widen(w[0] * inv) * natural(o0_ref, on_ref.at[0])
         + widen(w[1] * inv) * natural(o1_ref, on_ref.at[1])
         + widen(w[2] * inv) * natural(o2_ref, on_ref.at[2]))
    y_attn = _dot(o.astype(BF16), w_ao_ref[...])

    acc = cv_ref[...].astype(F32)
    mu = jnp.mean(acc, axis=-1, keepdims=True)
    xc = acc - mu
    y = xc * lax.rsqrt(jnp.mean(xc * xc, axis=-1, keepdims=True) + LN_EPS)
    y = _silu_of_twice(y * g_ln_ref[...] + b_ln_ref[...])
    y_conv = _dot(y.astype(BF16), w_co_ref[...]) + b_co_ref[...]

    merged = ga_ref[...].astype(F32) * y_attn + gc_ref[...].astype(F32) * y_conv
    mix = _dot(merged.astype(BF16), w_mo_ref[...])
    out_ref[...] = x_ref[...] + _rms(mix, g_pm_ref[...])


def _mixer(x2, cv, gates, os_, stats, params, tm=512):
    S = x2.shape[0]

    def row(w, col=0):
        return pl.BlockSpec((tm, w), lambda i: (i, col))

    def rspec(a):
        r, _, w = a.shape
        return pl.BlockSpec((r, tm // r, w), lambda i: (0, i, 0))

    return pl.pallas_call(
        _mixer_kernel,
        grid=(S // tm,),
        in_specs=[rspec(a) for a in os_] + [rspec(a) for a in stats]
                 + [row(CONV_CHANNELS), row(D_MODEL, 0), row(D_MODEL, 1), row(D_MODEL)]
                 + [_resident(p) for p in params],
        out_specs=row(D_MODEL),
        out_shape=jax.ShapeDtypeStruct((S, D_MODEL), F32),
        scratch_shapes=[pltpu.VMEM((N_GROUPS, GROUP_W // LANES, tm, LANES), F32),
                        pltpu.VMEM((N_GROUPS, LSE_W // LANES, tm, LANES), F32)],
        compiler_params=pltpu.CompilerParams(
            dimension_semantics=("parallel",), vmem_limit_bytes=VMEM_LIMIT),
        name="mixer",
    )(*os_, *stats, cv, gates, gates, x2, *params)


def _ffn_kernel(x_ref, g_pre_ref, w_in_ref, w_out_ref, g_post_ref, out_ref, *, chunk):
    x = x_ref[...]
    h = _rms(x, g_pre_ref[...]).astype(BF16)
    acc = jnp.zeros(x.shape, F32)
    for c0 in range(0, FFN_HIDDEN, chunk):
        half_gate = _dot(h, w_in_ref[:, c0:c0 + chunk])
        up = _dot(h, w_in_ref[:, FFN_HIDDEN + c0:FFN_HIDDEN + c0 + chunk])
        act = (_silu_of_twice(half_gate) * up).astype(BF16)
        acc = acc + _dot(act, w_out_ref[c0:c0 + chunk, :])
    out_ref[...] = x + _rms(acc, g_post_ref[...])


def _ffn(x1, g_pre, w_in, w_out, g_post, tm=512, chunk=1408):
    S = x1.shape[0]
    row = pl.BlockSpec((tm, D_MODEL), lambda i: (i, 0))
    params = (g_pre, w_in, w_out, g_post)
    return pl.pallas_call(
        functools.partial(_ffn_kernel, chunk=chunk),
        grid=(S // tm,),
        in_specs=[row] + [_resident(p) for p in params],
        out_specs=row,
        out_shape=jax.ShapeDtypeStruct((S, D_MODEL), F32),
        compiler_params=pltpu.CompilerParams(
            dimension_semantics=("parallel",), vmem_limit_bytes=VMEM_LIMIT),
        name="ffn",
    )(x1, *params)


def _lane_chunks(a):
    return a.reshape(a.shape[0], -1, LANES).transpose(1, 0, 2)


def _split_w_in(w):
    q, k, v, rest = (w[:, :ATTN_W], w[:, ATTN_W:2 * ATTN_W], w[:, 2 * ATTN_W:3 * ATTN_W], w[:, 3 * ATTN_W:])
    w_qkv = []
    for g in range(N_GROUPS):
        sl = slice(g * GROUP_W, (g + 1) * GROUP_W)
        w_qkv.append(jnp.concatenate([q[:, sl], k[:, sl], v[:, sl]], axis=1).astype(BF16))
    w_glu = _pair_glu_chunks(0.5 * rest[:, :2 * CONV_CHANNELS]).astype(BF16)
    w_gate = (0.5 * rest[:, 2 * CONV_CHANNELS:]).astype(BF16)
    return w_glu, w_gate, w_qkv


def _pair_glu_chunks(a):
    n = a.shape[0]
    return a.reshape(n, 2, CONV_CHANNELS // LANES, LANES).transpose(0, 2, 1, 3).reshape(n, 2 * CONV_CHANNELS)


def kernel(x, rel_bias_table, g_pre_mix, w_in, b_glu, w_dw, b_dw, g_conv_ln, b_conv_ln, w_conv_out, b_conv_out, w_attn_out, w_mix_out, g_post_mix, g_pre_ffn, w_ffn_in, w_ffn_out, g_post_ffn):
    B, S, D = x.shape
    depth = w_in.shape[0]
    expand = jnp.asarray(
        (np.arange(LSE_W)[:, None] == np.arange(GROUP_W)[None, :] // HEAD_DIM), dtype=BF16)
    biases = _bias_tables(rel_bias_table)
    outs = []
    for b in range(B):
        xb = x[b]
        for l in range(depth):
            r2 = lambda a: a[l].reshape(1, -1)
            w_glu, w_gate, w_qkv = _split_w_in(w_in[l])
            cv, gates, *qkvs = _in_proj(xb, r2(g_pre_mix), w_glu, _pair_glu_chunks(0.5 * r2(b_glu)), w_gate,
                                        w_qkv, _lane_chunks(w_dw[l]), _lane_chunks(r2(b_dw)))
            os_, stats = [], []
            for g, qkv in enumerate(qkvs):
                o_g, ml_g = _attention_group(qkv, biases, g)
                os_.append(o_g)
                stats.append(ml_g)
            params = (expand, 0.5 * r2(g_conv_ln), 0.5 * r2(b_conv_ln), w_conv_out[l].astype(BF16),
                      r2(b_conv_out), w_attn_out[l].astype(BF16), w_mix_out[l].astype(BF16), r2(g_post_mix))
            x1 = _mixer(xb, cv, gates, os_, stats, params)
            w_ffn = jnp.concatenate([0.5 * w_ffn_in[l][:, :FFN_HIDDEN], w_ffn_in[l][:, FFN_HIDDEN:]], axis=1)
            xb = _ffn(x1, r2(g_pre_ffn), w_ffn.astype(BF16), w_ffn_out[l].astype(BF16), r2(g_post_ffn))
        outs.append(xb)
    return jnp.stack(outs, axis=0)
```

```python
import functools
import math

import jax
import jax.numpy as jnp
import numpy as np
from jax import lax
from jax.experimental import pallas as pl
from jax.experimental.pallas import tpu as pltpu

D_MODEL = 1024
HEAD_DIM = 64
HEADS_PER_GROUP = 8
DILATED_GROUPS = ((128, 1), (512, 4), (2048, 16))
N_GROUPS = len(DILATED_GROUPS)
GROUP_W = HEADS_PER_GROUP * HEAD_DIM
ATTN_W = N_GROUPS * GROUP_W
QKV_W = 3 * GROUP_W
BLOCK = 128
REL_BUCKETS = 32
REL_MAX_DISTANCE = 2048
CONV_CHANNELS = D_MODEL
CONV_WIDTH = 31
FFN_HIDDEN = 2816
RMS_EPS = 1e-6
LN_EPS = 1e-5
NEG_INF = -1e30
LOG2E = math.log2(math.e)

LANES = 128
SUBLANES = 8
CONV_ROWS = 64
LSE_W = LANES
HALO = 32
VMEM_LIMIT = 56 * 1024 * 1024

BF16 = jnp.bfloat16
F32 = jnp.float32


def _sigmoid_of_twice(h):
    return 0.5 * jnp.tanh(h) + 0.5


def _silu_of_twice(h):
    return h * (jnp.tanh(h) + 1.0)


def _rms(v, g):
    return v * lax.rsqrt(jnp.mean(v * v, axis=-1, keepdims=True) + RMS_EPS) * g


def _dot(a, b):
    return jnp.dot(a, b, preferred_element_type=F32)


def _resident(a):
    return pl.BlockSpec(a.shape, lambda *_: (0,) * a.ndim, pipeline_mode=pl.Buffered(1))


def _causal_dwconv_chunk(k, ext_ref, sh_ref, w_ref, bias, out_ref):
    tm = out_ref.shape[0]
    ns = sh_ref.shape[1]
    ext = ext_ref.at[k]
    lanes = slice(k * LANES, (k + 1) * LANES)
    for s in range(1, SUBLANES):
        sh_ref[s - 1] = ext[s:s + ns, :]
    for r0 in range(0, tm, CONV_ROWS):
        acc = jnp.zeros((CONV_ROWS, LANES), F32) + bias
        for j in range(CONV_WIDTH):
            a, s = divmod(HALO - (CONV_WIDTH - 1) + j, SUBLANES)
            src = ext if s == 0 else sh_ref.at[s - 1]
            row = a * SUBLANES + r0
            acc = acc + src[row:row + CONV_ROWS, :] * w_ref[k, j:j + 1, :]
        out_ref[r0:r0 + CONV_ROWS, lanes] = acc.astype(out_ref.dtype)


def _inproj_kernel(x_ref, g_ref, w_glu_ref, b_glu_ref, w_gate_ref, w_qkv_ref,
                   w_dw_ref, b_dw_ref, cv_ref, gate_ref, q0_ref, q1_ref, q2_ref,
                   h_ref, xn_ref, ext_ref, sh_ref):
    nx = D_MODEL // LANES
    tm = x_ref.shape[0]
    C = CONV_CHANNELS

    @pl.when(pl.program_id(0) == 0)
    def _():
        ext_ref[:, 0:HALO, :] = jnp.zeros((C // LANES, HALO, LANES), F32)

    xn = _rms(x_ref[...], g_ref[...])
    h_ref[0] = xn.astype(BF16)
    for k in range(nx):
        xn_ref[k] = xn[:, k * LANES:(k + 1) * LANES]
    for gi, (_, r) in enumerate(DILATED_GROUPS):
        if r == 1:
            continue
        n = tm // r
        for c in range(r):
            rows = jnp.concatenate([xn_ref[k, pl.ds(c, n, stride=r), :] for k in range(nx)], axis=-1)
            h_ref[gi, c * n:(c + 1) * n, :] = rows.astype(BF16)

    for k in range(C // LANES):
        cols = slice(2 * k * LANES, 2 * (k + 1) * LANES)
        z = _dot(h_ref[0], w_glu_ref[:, cols]) + b_glu_ref[:, cols]
        ext_ref[k, HALO:HALO + tm, :] = z[:, :LANES] * (jnp.tanh(z[:, LANES:]) + 1.0)
        _causal_dwconv_chunk(k, ext_ref, sh_ref.at[k % 2], w_dw_ref, b_dw_ref[k], cv_ref)
    for k in range(C // LANES):
        ext_ref[k, 0:HALO, :] = ext_ref[k, tm:tm + HALO, :]

    gate_ref[...] = _sigmoid_of_twice(_dot(h_ref[0], w_gate_ref[...])).astype(BF16)
    for gi, q_ref in enumerate((q0_ref, q1_ref, q2_ref)):
        for part in range(3):
            c0 = part * ATTN_W + gi * GROUP_W
            z = _dot(h_ref[gi], w_qkv_ref[:, c0:c0 + GROUP_W]).astype(BF16)
            q_ref[:, :, part * GROUP_W:(part + 1) * GROUP_W] = z.reshape(q_ref.shape[:2] + (GROUP_W,))


def _in_proj(x2, g, w_glu, b_glu, w_gate, w_qkv, w_dw, b_dw, tm=512):
    S = x2.shape[0]
    assert DILATED_GROUPS[0][1] == 1
    params = (g, w_glu, b_glu, w_gate, w_qkv, w_dw, b_dw)
    dil = [d for _, d in DILATED_GROUPS]
    return pl.pallas_call(
        _inproj_kernel,
        grid=(S // tm,),
        in_specs=[pl.BlockSpec((tm, D_MODEL), lambda i: (i, 0))] + [_resident(p) for p in params],
        out_specs=[pl.BlockSpec((tm, CONV_CHANNELS), lambda i: (i, 0)),
                   pl.BlockSpec((tm, 2 * D_MODEL), lambda i: (i, 0))]
                  + [pl.BlockSpec((r, tm // r, QKV_W), lambda i: (0, i, 0)) for r in dil],
        out_shape=[jax.ShapeDtypeStruct((S, CONV_CHANNELS), BF16),
                   jax.ShapeDtypeStruct((S, 2 * D_MODEL), BF16)]
                  + [jax.ShapeDtypeStruct((r, S // r, QKV_W), BF16) for r in dil],
        scratch_shapes=[pltpu.VMEM((N_GROUPS, tm, D_MODEL), BF16),
                        pltpu.VMEM((D_MODEL // LANES, tm, LANES), F32),
                        pltpu.VMEM((CONV_CHANNELS // LANES, HALO + tm, LANES), F32),
                        pltpu.VMEM((2, SUBLANES - 1, HALO + tm - SUBLANES, LANES), F32)],
        compiler_params=pltpu.CompilerParams(
            dimension_semantics=("arbitrary",), vmem_limit_bytes=VMEM_LIMIT),
        name="in_proj",
    )(x2, *params)


def _bucket_maps():
    max_exact = REL_BUCKETS // 2
    a = np.arange(BLOCK, dtype=np.int64)[:, None]
    c = np.arange(2 * BLOCK, dtype=np.int64)[None, :]
    offset = a - c + BLOCK
    maps = []
    for window, dilation in DILATED_GROUPS:
        span = window // dilation
        assert span == BLOCK
        d = np.maximum(offset * dilation, 0)
        df = np.maximum(d, 1).astype(np.float32)
        large = max_exact + (np.log(df / np.float32(max_exact)) / np.float32(math.log(REL_MAX_DISTANCE / max_exact))
                             * np.float32(REL_BUCKETS - max_exact)).astype(np.int32)
        large = np.minimum(large, REL_BUCKETS - 1)
        bucket = np.where(d < max_exact, d, large)
        valid = (offset >= 0) & (offset <= span)
        maps.append(np.where(valid, bucket, -1).astype(np.int32))
    return np.stack(maps)


def _bias_kernel(tab_ref, bucket_ref, o_ref):
    g = pl.program_id(0)
    bucket = bucket_ref[...]
    col = lax.broadcasted_iota(jnp.int32, bucket.shape, 1)
    in_bucket = [bucket == b for b in range(REL_BUCKETS)]
    for h in range(HEADS_PER_GROUP):
        acc = jnp.full(bucket.shape, NEG_INF, F32)
        for b in range(REL_BUCKETS):
            acc = jnp.where(in_bucket[b], tab_ref[b, g * HEADS_PER_GROUP + h] * LOG2E, acc)
        o_ref[0, h] = acc
        o_ref[1, h] = jnp.where(col < BLOCK, NEG_INF, acc)


def _bias_tables(rel_bias_table):
    return pl.pallas_call(
        _bias_kernel,
        grid=(N_GROUPS,),
        in_specs=[
            pl.BlockSpec(memory_space=pltpu.SMEM),
            pl.BlockSpec((None, BLOCK, 2 * BLOCK), lambda g: (g, 0, 0)),
        ],
        out_specs=pl.BlockSpec((None, 2, HEADS_PER_GROUP, BLOCK, 2 * BLOCK), lambda g: (g, 0, 0, 0, 0)),
        out_shape=jax.ShapeDtypeStruct((N_GROUPS, 2, HEADS_PER_GROUP, BLOCK, 2 * BLOCK), F32),
        name="bias_tables",
    )(rel_bias_table.astype(F32), jnp.asarray(_bucket_maps()))


def _attn_kernel(q_ref, kp_ref, kc_ref, vp_ref, vc_ref, bias_ref, sumcol_ref, o_ref, ml_ref):
    first = jnp.where(pl.program_id(1) == 0, 1, 0)
    lane = lax.broadcasted_iota(jnp.int32, (BLOCK, LANES), 1)
    low = lane < HEAD_DIM
    for j in range(q_ref.shape[0] // BLOCK):
        rows = slice(j * BLOCK, (j + 1) * BLOCK)
        variant = first if j == 0 else 0
        ml = jnp.zeros((BLOCK, LANES), F32)
        for pair in range(HEADS_PER_GROUP // 2):
            sl = slice(pair * LANES, (pair + 1) * LANES)
            q = q_ref[rows, sl] * jnp.asarray(HEAD_DIM ** -0.5, BF16)
            if j == 0:
                k = jnp.concatenate([kp_ref[:, sl], kc_ref[rows, sl]], axis=0)
                v = jnp.concatenate([vp_ref[:, sl], vc_ref[rows, sl]], axis=0)
            else:
                k = kc_ref[(j - 1) * BLOCK:(j + 1) * BLOCK, sl]
                v = vc_ref[(j - 1) * BLOCK:(j + 1) * BLOCK, sl]
            halves = []
            for sub in range(2):
                h = 2 * pair + sub
                qm = jnp.where(low if sub == 0 else ~low, q, jnp.zeros_like(q))
                s = lax.dot_general(qm, k, (((1,), (1,)), ((), ())), preferred_element_type=F32)
                t = s * LOG2E + bias_ref[variant, h]
                mt = jnp.max(t, axis=-1, keepdims=True)
                p = jnp.exp2(t - mt).astype(BF16)
                r = _dot(p, jnp.concatenate([v, sumcol_ref[h]], axis=1))
                halves.append(r[:, :LANES])
                ml = jnp.where(lane == h, mt, ml) + r[:, LANES:]
            o_ref[rows, sl] = jnp.where(low, halves[0], halves[1]).astype(o_ref.dtype)
        ml_ref[rows, :] = ml


def _attention_group(qkv, biases, g, nq=8):
    r, L, _ = qkv.shape
    tq = nq * BLOCK
    sumcol = jnp.asarray(np.broadcast_to(
        np.arange(LANES)[None, None, :] == HEADS_PER_GROUP + np.arange(HEADS_PER_GROUP)[:, None, None],
        (HEADS_PER_GROUP, 2 * BLOCK, LANES)), dtype=BF16)

    def spec(part, prev):
        if prev:
            return pl.BlockSpec((None, BLOCK, GROUP_W), lambda c, i: (c, jnp.maximum(i * nq - 1, 0), part))
        return pl.BlockSpec((None, tq, GROUP_W), lambda c, i: (c, i, part))

    return pl.pallas_call(
        _attn_kernel,
        grid=(r, L // tq),
        in_specs=[
            spec(0, False),
            spec(1, True), spec(1, False),
            spec(2, True), spec(2, False),
            pl.BlockSpec((None, 2, HEADS_PER_GROUP, BLOCK, 2 * BLOCK), lambda c, i: (g, 0, 0, 0, 0)),
            pl.BlockSpec(sumcol.shape, lambda c, i: (0, 0, 0)),
        ],
        out_specs=[
            pl.BlockSpec((None, tq, GROUP_W), lambda c, i: (c, i, 0)),
            pl.BlockSpec((None, tq, LSE_W), lambda c, i: (c, i, 0)),
        ],
        out_shape=[
            jax.ShapeDtypeStruct((r, L, GROUP_W), BF16),
            jax.ShapeDtypeStruct((r, L, LSE_W), F32),
        ],
        compiler_params=pltpu.CompilerParams(
            dimension_semantics=("parallel", "arbitrary"), vmem_limit_bytes=VMEM_LIMIT),
        name=f"attn_g{g}",
    )(qkv, qkv, qkv, qkv, qkv, biases, sumcol)


def _mixer_kernel(o0_ref, o1_ref, o2_ref, l0_ref, l1_ref, l2_ref, cv_ref, ga_ref, gc_ref, x_ref,
                  expand_ref, g_ln_ref, b_ln_ref, w_co_ref, b_co_ref, w_ao_ref, w_mo_ref, g_pm_ref,
                  out_ref, on_ref, ln_ref):
    def natural(src_ref, dst_ref):
        r, n, w = src_ref.shape
        if r == 1:
            return src_ref[0].astype(F32)
        for c in range(r):
            for k in range(w // LANES):
                dst_ref[k, pl.ds(c, n, stride=r), :] = src_ref[c, :, k * LANES:(k + 1) * LANES].astype(F32)
        return jnp.concatenate([dst_ref[k] for k in range(w // LANES)], axis=-1)

    ml = [natural(r_, ln_ref.at[g]) for g, r_ in enumerate((l0_ref, l1_ref, l2_ref))]
    den = [pltpu.roll(a, LANES - HEADS_PER_GROUP, axis=1) for a in ml]
    mx = jnp.maximum(jnp.maximum(ml[0], ml[1]), ml[2])
    w = [jnp.exp2(a - mx) for a in ml]
    total = w[0] * den[0] + w[1] * den[1] + w[2] * den[2]
    head_lane = lax.broadcasted_iota(jnp.int32, total.shape, 1) < HEADS_PER_GROUP
    inv = jnp.where(head_lane, 1.0 / total, 0.0)
    expand = expand_ref[...]

    def widen(a):
        hi = a.astype(BF16)
        lo = (a - hi.astype(F32)).astype(BF16)
        return _dot(hi, expand) + _dot(lo, expand)

    o = (widen(w[0] * inv) * natural(o0_ref, on_ref.at[0])
         + widen(w[1] * inv) * natural(o1_ref, on_ref.at[1])
         + widen(w[2] * inv) * natural(o2_ref, on_ref.at[2]))
    y_attn = _dot(o.astype(BF16), w_ao_ref[...])

    acc = cv_ref[...].astype(F32)
    mu = jnp.mean(acc, axis=-1, keepdims=True)
    xc = acc - mu
    y = xc * lax.rsqrt(jnp.mean(xc * xc, axis=-1, keepdims=True) + LN_EPS)
    y = _silu_of_twice(y * g_ln_ref[...] + b_ln_ref[...])
    y_conv = _dot(y.astype(BF16), w_co_ref[...]) + b_co_ref[...]

    merged = ga_ref[...].astype(F32) * y_attn + gc_ref[...].astype(F32) * y_conv
    mix = _dot(merged.astype(BF16), w_mo_ref[...])
    out_ref[...] = x_ref[...] + _rms(mix, g_pm_ref[...])


def _mixer(x2, cv, gates, os_, stats, params, tm=512):
    S = x2.shape[0]

    def row(w, col=0):
        return pl.BlockSpec((tm, w), lambda i: (i, col))

    def rspec(a):
        r, _, w = a.shape
        return pl.BlockSpec((r, tm // r, w), lambda i: (0, i, 0))

    return pl.pallas_call(
        _mixer_kernel,
        grid=(S // tm,),
        in_specs=[rspec(a) for a in os_] + [rspec(a) for a in stats]
                 + [row(CONV_CHANNELS), row(D_MODEL, 0), row(D_MODEL, 1), row(D_MODEL)]
                 + [_resident(p) for p in params],
        out_specs=row(D_MODEL),
        out_shape=jax.ShapeDtypeStruct((S, D_MODEL), F32),
        scratch_shapes=[pltpu.VMEM((N_GROUPS, GROUP_W // LANES, tm, LANES), F32),
                        pltpu.VMEM((N_GROUPS, LSE_W // LANES, tm, LANES), F32)],
        compiler_params=pltpu.CompilerParams(
            dimension_semantics=("parallel",), vmem_limit_bytes=VMEM_LIMIT),
        name="mixer",
    )(*os_, *stats, cv, gates, gates, x2, *params)


def _ffn_kernel(x_ref, g_pre_ref, w_in_ref, w_out_ref, g_post_ref, out_ref, *, chunk):
    x = x_ref[...]
    h = _rms(x, g_pre_ref[...]).astype(BF16)
    acc = jnp.zeros(x.shape, F32)
    for c0 in range(0, FFN_HIDDEN, chunk):
        half_gate = _dot(h, w_in_ref[:, c0:c0 + chunk])
        up = _dot(h, w_in_ref[:, FFN_HIDDEN + c0:FFN_HIDDEN + c0 + chunk])
        act = (_silu_of_twice(half_gate) * up).astype(BF16)
        acc = acc + _dot(act, w_out_ref[c0:c0 + chunk, :])
    out_ref[...] = x + _rms(acc, g_post_ref[...])


def _ffn(x1, g_pre, w_in, w_out, g_post, tm=512, chunk=1408):
    S = x1.shape[0]
    row = pl.BlockSpec((tm, D_MODEL), lambda i: (i, 0))
    params = (g_pre, w_in, w_out, g_post)
    return pl.pallas_call(
        functools.partial(_ffn_kernel, chunk=chunk),
        grid=(S // tm,),
        in_specs=[row] + [_resident(p) for p in params],
        out_specs=row,
        out_shape=jax.ShapeDtypeStruct((S, D_MODEL), F32),
        compiler_params=pltpu.CompilerParams(
            dimension_semantics=("parallel",), vmem_limit_bytes=VMEM_LIMIT),
        name="ffn",
    )(x1, *params)


def _lane_chunks(a):
    return a.reshape(a.shape[0], -1, LANES).transpose(1, 0, 2)


def _split_w_in(w):
    w_qkv = w[:, :3 * ATTN_W].astype(BF16)
    w_glu = _pair_glu_chunks((0.5 * w[:, 3 * ATTN_W:3 * ATTN_W + 2 * CONV_CHANNELS]).astype(BF16))
    w_gate = (0.5 * w[:, 3 * ATTN_W + 2 * CONV_CHANNELS:]).astype(BF16)
    return w_glu, w_gate, w_qkv


def _pair_glu_chunks(a):
    n = a.shape[0]
    return a.reshape(n, 2, CONV_CHANNELS // LANES, LANES).transpose(0, 2, 1, 3).reshape(n, 2 * CONV_CHANNELS)


def kernel(x, rel_bias_table, g_pre_mix, w_in, b_glu, w_dw, b_dw, g_conv_ln, b_conv_ln, w_conv_out, b_conv_out, w_attn_out, w_mix_out, g_post_mix, g_pre_ffn, w_ffn_in, w_ffn_out, g_post_ffn):
    B, S, D = x.shape
    depth = w_in.shape[0]
    expand = jnp.asarray(
        (np.arange(LSE_W)[:, None] == np.arange(GROUP_W)[None, :] // HEAD_DIM), dtype=BF16)
    biases = _bias_tables(rel_bias_table)
    outs = []
    for b in range(B):
        xb = x[b]
        for l in range(depth):
            r2 = lambda a: a[l].reshape(1, -1)
            w_glu, w_gate, w_qkv = _split_w_in(w_in[l])
            cv, gates, *qkvs = _in_proj(xb, r2(g_pre_mix), w_glu, _pair_glu_chunks(0.5 * r2(b_glu)), w_gate,
                                        w_qkv, _lane_chunks(w_dw[l]), _lane_chunks(r2(b_dw)))
            os_, stats = [], []
            for g, qkv in enumerate(qkvs):
                o_g, ml_g = _attention_group(qkv, biases, g)
                os_.append(o_g)
                stats.append(ml_g)
            params = (expand, 0.5 * r2(g_conv_ln), 0.5 * r2(b_conv_ln), w_conv_out[l].astype(BF16),
                      r2(b_conv_out), w_attn_out[l].astype(BF16), w_mix_out[l].astype(BF16), r2(g_post_mix))
            x1 = _mixer(xb, cv, gates, os_, stats, params)
            gate_half = jnp.asarray(np.where(np.arange(2 * FFN_HIDDEN) < FFN_HIDDEN, 0.5, 1.0), F32)
            w_ffn = (w_ffn_in[l] * gate_half).astype(BF16)
            xb = _ffn(x1, r2(g_pre_ffn), w_ffn, w_ffn_out[l].astype(BF16), r2(g_post_ffn))
        outs.append(xb)
    return jnp.stack(outs, axis=0)
```

```python
import functools
import math

import jax
import jax.numpy as jnp
import numpy as np
from jax import lax
from jax.experimental import pallas as pl
from jax.experimental.pallas import tpu as pltpu

D_MODEL = 1024
HEAD_DIM = 64
HEADS_PER_GROUP = 8
DILATED_GROUPS = ((128, 1), (512, 4), (2048, 16))
N_GROUPS = len(DILATED_GROUPS)
GROUP_W = HEADS_PER_GROUP * HEAD_DIM
ATTN_W = N_GROUPS * GROUP_W
QKV_W = 3 * GROUP_W
BLOCK = 128
REL_BUCKETS = 32
REL_MAX_DISTANCE = 2048
CONV_CHANNELS = D_MODEL
CONV_WIDTH = 31
FFN_HIDDEN = 2816
RMS_EPS = 1e-6
LN_EPS = 1e-5
NEG_INF = -1e30
LOG2E = math.log2(math.e)
GLU_COL = 3 * ATTN_W
GATE_COL = GLU_COL + 2 * CONV_CHANNELS

LANES = 128
SUBLANES = 8
CONV_ROWS = 64
LSE_W = LANES
HALO = 32
VMEM_LIMIT = 56 * 1024 * 1024

BF16 = jnp.bfloat16
F32 = jnp.float32


def _sigmoid_of_twice(h):
    return 0.5 * jnp.tanh(h) + 0.5


def _silu_of_twice(h):
    return h * (jnp.tanh(h) + 1.0)


def _rms(v, g):
    return v * lax.rsqrt(jnp.mean(v * v, axis=-1, keepdims=True) + RMS_EPS) * g


def _dot(a, b):
    return jnp.dot(a, b, preferred_element_type=F32)


def _resident(a):
    return pl.BlockSpec(a.shape, lambda *_: (0,) * a.ndim, pipeline_mode=pl.Buffered(1))


def _causal_dwconv_chunk(k, ext_ref, sh_ref, w_ref, bias, out_ref):
    tm = out_ref.shape[0]
    ns = sh_ref.shape[1]
    ext = ext_ref.at[k]
    lanes = slice(k * LANES, (k + 1) * LANES)
    for s in range(1, SUBLANES):
        sh_ref[s - 1] = ext[s:s + ns, :]
    for r0 in range(0, tm, CONV_ROWS):
        acc = jnp.zeros((CONV_ROWS, LANES), F32) + bias
        for j in range(CONV_WIDTH):
            a, s = divmod(HALO - (CONV_WIDTH - 1) + j, SUBLANES)
            src = ext if s == 0 else sh_ref.at[s - 1]
            row = a * SUBLANES + r0
            acc = acc + src[row:row + CONV_ROWS, :] * w_ref[k, j:j + 1, :]
        out_ref[r0:r0 + CONV_ROWS, lanes] = acc.astype(out_ref.dtype)


def _inproj_kernel(x_ref, g_ref, w_all_ref, w_glu_ref, b_glu_ref,
                   w_dw_ref, b_dw_ref, cv_ref, gate_ref, q0_ref, q1_ref, q2_ref,
                   h_ref, xn_ref, ext_ref, sh_ref):
    nx = D_MODEL // LANES
    tm = x_ref.shape[0]
    C = CONV_CHANNELS

    @pl.when(pl.program_id(0) == 0)
    def _():
        ext_ref[:, 0:HALO, :] = jnp.zeros((C // LANES, HALO, LANES), F32)

    xn = _rms(x_ref[...], g_ref[...])
    h_ref[0] = xn.astype(BF16)
    for k in range(nx):
        xn_ref[k] = xn[:, k * LANES:(k + 1) * LANES]
    for gi, (_, r) in enumerate(DILATED_GROUPS):
        if r == 1:
            continue
        n = tm // r
        for c in range(r):
            rows = jnp.concatenate([xn_ref[k, pl.ds(c, n, stride=r), :] for k in range(nx)], axis=-1)
            h_ref[gi, c * n:(c + 1) * n, :] = rows.astype(BF16)

    for k in range(C // LANES):
        cols = slice(2 * k * LANES, 2 * (k + 1) * LANES)
        z = _dot(h_ref[0], w_glu_ref[:, cols]) + b_glu_ref[:, cols]
        ext_ref[k, HALO:HALO + tm, :] = z[:, :LANES] * (jnp.tanh(z[:, LANES:]) + 1.0)
        _causal_dwconv_chunk(k, ext_ref, sh_ref.at[k % 2], w_dw_ref, b_dw_ref[k], cv_ref)
    for k in range(C // LANES):
        ext_ref[k, 0:HALO, :] = ext_ref[k, tm:tm + HALO, :]

    gate_ref[...] = _sigmoid_of_twice(_dot(h_ref[0], w_all_ref[:, GATE_COL:])).astype(BF16)
    for gi, q_ref in enumerate((q0_ref, q1_ref, q2_ref)):
        for part in range(3):
            c0 = part * ATTN_W + gi * GROUP_W
            z = _dot(h_ref[gi], w_all_ref[:, c0:c0 + GROUP_W]).astype(BF16)
            q_ref[:, :, part * GROUP_W:(part + 1) * GROUP_W] = z.reshape(q_ref.shape[:2] + (GROUP_W,))


def _in_proj(x2, g, w_all, w_glu, b_glu, w_dw, b_dw, tm=512):
    S = x2.shape[0]
    assert DILATED_GROUPS[0][1] == 1
    params = (g, w_all, w_glu, b_glu, w_dw, b_dw)
    dil = [d for _, d in DILATED_GROUPS]
    return pl.pallas_call(
        _inproj_kernel,
        grid=(S // tm,),
        in_specs=[pl.BlockSpec((tm, D_MODEL), lambda i: (i, 0))] + [_resident(p) for p in params],
        out_specs=[pl.BlockSpec((tm, CONV_CHANNELS), lambda i: (i, 0)),
                   pl.BlockSpec((tm, 2 * D_MODEL), lambda i: (i, 0))]
                  + [pl.BlockSpec((r, tm // r, QKV_W), lambda i: (0, i, 0)) for r in dil],
        out_shape=[jax.ShapeDtypeStruct((S, CONV_CHANNELS), BF16),
                   jax.ShapeDtypeStruct((S, 2 * D_MODEL), BF16)]
                  + [jax.ShapeDtypeStruct((r, S // r, QKV_W), BF16) for r in dil],
        scratch_shapes=[pltpu.VMEM((N_GROUPS, tm, D_MODEL), BF16),
                        pltpu.VMEM((D_MODEL // LANES, tm, LANES), F32),
                        pltpu.VMEM((CONV_CHANNELS // LANES, HALO + tm, LANES), F32),
                        pltpu.VMEM((2, SUBLANES - 1, HALO + tm - SUBLANES, LANES), F32)],
        compiler_params=pltpu.CompilerParams(
            dimension_semantics=("arbitrary",), vmem_limit_bytes=VMEM_LIMIT),
        name="in_proj",
    )(x2, *params)


def _bucket_maps():
    max_exact = REL_BUCKETS // 2
    a = np.arange(BLOCK, dtype=np.int64)[:, None]
    c = np.arange(2 * BLOCK, dtype=np.int64)[None, :]
    offset = a - c + BLOCK
    maps = []
    for window, dilation in DILATED_GROUPS:
        span = window // dilation
        assert span == BLOCK
        d = np.maximum(offset * dilation, 0)
        df = np.maximum(d, 1).astype(np.float32)
        large = max_exact + (np.log(df / np.float32(max_exact)) / np.float32(math.log(REL_MAX_DISTANCE / max_exact))
                             * np.float32(REL_BUCKETS - max_exact)).astype(np.int32)
        large = np.minimum(large, REL_BUCKETS - 1)
        bucket = np.where(d < max_exact, d, large)
        valid = (offset >= 0) & (offset <= span)
        maps.append(np.where(valid, bucket, -1).astype(np.int32))
    return np.stack(maps)


def _bias_kernel(tab_ref, bucket_ref, o_ref):
    g = pl.program_id(0)
    bucket = bucket_ref[...]
    col = lax.broadcasted_iota(jnp.int32, bucket.shape, 1)
    in_bucket = [bucket == b for b in range(REL_BUCKETS)]
    for h in range(HEADS_PER_GROUP):
        acc = jnp.full(bucket.shape, NEG_INF, F32)
        for b in range(REL_BUCKETS):
            acc = jnp.where(in_bucket[b], tab_ref[b, g * HEADS_PER_GROUP + h] * LOG2E, acc)
        o_ref[0, h] = acc
        o_ref[1, h] = jnp.where(col < BLOCK, NEG_INF, acc)


def _bias_tables(rel_bias_table):
    return pl.pallas_call(
        _bias_kernel,
        grid=(N_GROUPS,),
        in_specs=[
            pl.BlockSpec(memory_space=pltpu.SMEM),
            pl.BlockSpec((None, BLOCK, 2 * BLOCK), lambda g: (g, 0, 0)),
        ],
        out_specs=pl.BlockSpec((None, 2, HEADS_PER_GROUP, BLOCK, 2 * BLOCK), lambda g: (g, 0, 0, 0, 0)),
        out_shape=jax.ShapeDtypeStruct((N_GROUPS, 2, HEADS_PER_GROUP, BLOCK, 2 * BLOCK), F32),
        name="bias_tables",
    )(rel_bias_table.astype(F32), jnp.asarray(_bucket_maps()))


def _attn_kernel(q_ref, kp_ref, kc_ref, vp_ref, vc_ref, bias_ref, sumcol_ref, o_ref, ml_ref):
    first = jnp.where(pl.program_id(1) == 0, 1, 0)
    lane = lax.broadcasted_iota(jnp.int32, (BLOCK, LANES), 1)
    low = lane < HEAD_DIM
    for j in range(q_ref.shape[0] // BLOCK):
        rows = slice(j * BLOCK, (j + 1) * BLOCK)
        variant = first if j == 0 else 0
        ml = jnp.zeros((BLOCK, LANES), F32)
        for pair in range(HEADS_PER_GROUP // 2):
            sl = slice(pair * LANES, (pair + 1) * LANES)
            q = q_ref[rows, sl] * jnp.asarray(HEAD_DIM ** -0.5, BF16)
            if j == 0:
                k = jnp.concatenate([kp_ref[:, sl], kc_ref[rows, sl]], axis=0)
                v = jnp.concatenate([vp_ref[:, sl], vc_ref[rows, sl]], axis=0)
            else:
                k = kc_ref[(j - 1) * BLOCK:(j + 1) * BLOCK, sl]
                v = vc_ref[(j - 1) * BLOCK:(j + 1) * BLOCK, sl]
            halves = []
            for sub in range(2):
                h = 2 * pair + sub
                qm = jnp.where(low if sub == 0 else ~low, q, jnp.zeros_like(q))
                s = lax.dot_general(qm, k, (((1,), (1,)), ((), ())), preferred_element_type=F32)
                t = s * LOG2E + bias_ref[variant, h]
                mt = jnp.max(t, axis=-1, keepdims=True)
                p = jnp.exp2(t - mt).astype(BF16)
                r = _dot(p, jnp.concatenate([v, sumcol_ref[h]], axis=1))
                halves.append(r[:, :LANES])
                ml = jnp.where(lane == h, mt, ml) + r[:, LANES:]
            o_ref[rows, sl] = jnp.where(low, halves[0], halves[1]).astype(o_ref.dtype)
        ml_ref[rows, :] = ml


def _attention_group(qkv, biases, g, nq=8):
    r, L, _ = qkv.shape
    tq = nq * BLOCK
    sumcol = jnp.asarray(np.broadcast_to(
        np.arange(LANES)[None, None, :] == HEADS_PER_GROUP + np.arange(HEADS_PER_GROUP)[:, None, None],
        (HEADS_PER_GROUP, 2 * BLOCK, LANES)), dtype=BF16)

    def spec(part, prev):
        if prev:
            return pl.BlockSpec((None, BLOCK, GROUP_W), lambda c, i: (c, jnp.maximum(i * nq - 1, 0), part))
        return pl.BlockSpec((None, tq, GROUP_W), lambda c, i: (c, i, part))

    return pl.pallas_call(
        _attn_kernel,
        grid=(r, L // tq),
        in_specs=[
            spec(0, False),
            spec(1, True), spec(1, False),
            spec(2, True), spec(2, False),
            pl.BlockSpec((None, 2, HEADS_PER_GROUP, BLOCK, 2 * BLOCK), lambda c, i: (g, 0, 0, 0, 0)),
            pl.BlockSpec(sumcol.shape, lambda c, i: (0, 0, 0)),
        ],
        out_specs=[
            pl.BlockSpec((None, tq, GROUP_W), lambda c, i: (c, i, 0)),
            pl.BlockSpec((None, tq, LSE_W), lambda c, i: (c, i, 0)),
        ],
        out_shape=[
            jax.ShapeDtypeStruct((r, L, GROUP_W), BF16),
            jax.ShapeDtypeStruct((r, L, LSE_W), F32),
        ],
        compiler_params=pltpu.CompilerParams(
            dimension_semantics=("parallel", "arbitrary"), vmem_limit_bytes=VMEM_LIMIT),
        name=f"attn_g{g}",
    )(qkv, qkv, qkv, qkv, qkv, biases, sumcol)


def _mixer_kernel(o0_ref, o1_ref, o2_ref, l0_ref, l1_ref, l2_ref, cv_ref, ga_ref, gc_ref, x_ref,
                  expand_ref, g_ln_ref, b_ln_ref, w_co_ref, b_co_ref, w_ao_ref, w_mo_ref, g_pm_ref,
                  out_ref, on_ref, ln_ref):
    def natural(src_ref, dst_ref):
        r, n, w = src_ref.shape
        if r == 1:
            return src_ref[0].astype(F32)
        for c in range(r):
            for k in range(w // LANES):
                dst_ref[k, pl.ds(c, n, stride=r), :] = src_ref[c, :, k * LANES:(k + 1) * LANES].astype(F32)
        return jnp.concatenate([dst_ref[k] for k in range(w // LANES)], axis=-1)

    ml = [natural(r_, ln_ref.at[g]) for g, r_ in enumerate((l0_ref, l1_ref, l2_ref))]
    den = [pltpu.roll(a, LANES - HEADS_PER_GROUP, axis=1) for a in ml]
    mx = jnp.maximum(jnp.maximum(ml[0], ml[1]), ml[2])
    w = [jnp.exp2(a - mx) for a in ml]
    total = w[0] * den[0] + w[1] * den[1] + w[2] * den[2]
    head_lane = lax.broadcasted_iota(jnp.int32, total.shape, 1) < HEADS_PER_GROUP
    inv = jnp.where(head_lane, 1.0 / total, 0.0)
    expand = expand_ref[...]

    def widen(a):
        hi = a.astype(BF16)
        lo = (a - hi.astype(F32)).astype(BF16)
        return _dot(hi, expand) + _dot(lo, expand)

    o = (widen(w[0] * inv) * natural(o0_ref, on_ref.at[0])
         + widen(w[1] * inv) * natural(o1_ref, on_ref.at[1])
         + widen(w[2] * inv) * natural(o2_ref, on_ref.at[2]))
    y_attn = _dot(o.astype(BF16), w_ao_ref[...])

    acc = cv_ref[...].astype(F32)
    mu = jnp.mean(acc, axis=-1, keepdims=True)
    xc = acc - mu
    y = xc * lax.rsqrt(jnp.mean(xc * xc, axis=-1, keepdims=True) + LN_EPS)
    y = _silu_of_twice(y * g_ln_ref[...] + b_ln_ref[...])
    y_conv = _dot(y.astype(BF16), w_co_ref[...]) + b_co_ref[...]

    merged = ga_ref[...].astype(F32) * y_attn + gc_ref[...].astype(F32) * y_conv
    mix = _dot(merged.astype(BF16), w_mo_ref[...])
    out_ref[...] = x_ref[...] + _rms(mix, g_pm_ref[...])


def _mixer(x2, cv, gates, os_, stats, params, tm=512):
    S = x2.shape[0]

    def row(w, col=0):
        return pl.BlockSpec((tm, w), lambda i: (i, col))

    def rspec(a):
        r, _, w = a.shape
        return pl.BlockSpec((r, tm // r, w), lambda i: (0, i, 0))

    return pl.pallas_call(
        _mixer_kernel,
        grid=(S // tm,),
        in_specs=[rspec(a) for a in os_] + [rspec(a) for a in stats]
                 + [row(CONV_CHANNELS), row(D_MODEL, 0), row(D_MODEL, 1), row(D_MODEL)]
                 + [_resident(p) for p in params],
        out_specs=row(D_MODEL),
        out_shape=jax.ShapeDtypeStruct((S, D_MODEL), F32),
        scratch_shapes=[pltpu.VMEM((N_GROUPS, GROUP_W // LANES, tm, LANES), F32),
                        pltpu.VMEM((N_GROUPS, LSE_W // LANES, tm, LANES), F32)],
        compiler_params=pltpu.CompilerParams(
            dimension_semantics=("parallel",), vmem_limit_bytes=VMEM_LIMIT),
        name="mixer",
    )(*os_, *stats, cv, gates, gates, x2, *params)


def _ffn_kernel(x_ref, g_pre_ref, w_in_ref, w_out_ref, g_post_ref, out_ref, *, chunk):
    x = x_ref[...]
    h = _rms(x, g_pre_ref[...]).astype(BF16)
    acc = jnp.zeros(x.shape, F32)
    for c0 in range(0, FFN_HIDDEN, chunk):
        half_gate = _dot(h, w_in_ref[:, c0:c0 + chunk])
        up = _dot(h, w_in_ref[:, FFN_HIDDEN + c0:FFN_HIDDEN + c0 + chunk])
        act = (_silu_of_twice(half_gate) * up).astype(BF16)
        acc = acc + _dot(act, w_out_ref[c0:c0 + chunk, :])
    out_ref[...] = x + _rms(acc, g_post_ref[...])


def _ffn(x1, g_pre, w_in, w_out, g_post, tm=512, chunk=1408):
    S = x1.shape[0]
    row = pl.BlockSpec((tm, D_MODEL), lambda i: (i, 0))
    params = (g_pre, w_in, w_out, g_post)
    return pl.pallas_call(
        functools.partial(_ffn_kernel, chunk=chunk),
        grid=(S // tm,),
        in_specs=[row] + [_resident(p) for p in params],
        out_specs=row,
        out_shape=jax.ShapeDtypeStruct((S, D_MODEL), F32),
        compiler_params=pltpu.CompilerParams(
            dimension_semantics=("parallel",), vmem_limit_bytes=VMEM_LIMIT),
        name="ffn",
    )(x1, *params)


def _lane_chunks(a):
    return a.reshape(a.shape[0], -1, LANES).transpose(1, 0, 2)


def _split_w_in(w):
    half_after_qkv = jnp.asarray(np.where(np.arange(w.shape[1]) < 3 * ATTN_W, 1.0, 0.5), F32)
    w_all = (w * half_after_qkv).astype(BF16)
    w_glu = _pair_glu_chunks(w_all[:, GLU_COL:GLU_COL + 2 * CONV_CHANNELS])
    return w_all, w_glu


def _pair_glu_chunks(a):
    n = a.shape[0]
    return a.reshape(n, 2, CONV_CHANNELS // LANES, LANES).transpose(0, 2, 1, 3).reshape(n, 2 * CONV_CHANNELS)


def kernel(x, rel_bias_table, g_pre_mix, w_in, b_glu, w_dw, b_dw, g_conv_ln, b_conv_ln, w_conv_out, b_conv_out, w_attn_out, w_mix_out, g_post_mix, g_pre_ffn, w_ffn_in, w_ffn_out, g_post_ffn):
    B, S, D = x.shape
    depth = w_in.shape[0]
    expand = jnp.asarray(
        (np.arange(LSE_W)[:, None] == np.arange(GROUP_W)[None, :] // HEAD_DIM), dtype=BF16)
    biases = _bias_tables(rel_bias_table)
    outs = []
    for b in range(B):
        xb = x[b]
        for l in range(depth):
            r2 = lambda a: a[l].reshape(1, -1)
            w_all, w_glu = _split_w_in(w_in[l])
            cv, gates, *qkvs = _in_proj(xb, r2(g_pre_mix), w_all, w_glu, _pair_glu_chunks(0.5 * r2(b_glu)),
                                        _lane_chunks(w_dw[l]), _lane_chunks(r2(b_dw)))
            os_, stats = [], []
            for g, qkv in enumerate(qkvs):
                o_g, ml_g = _attention_group(qkv, biases, g)
                os_.append(o_g)
                stats.append(ml_g)
            params = (expand, 0.5 * r2(g_conv_ln), 0.5 * r2(b_conv_ln), w_conv_out[l].astype(BF16),
                      r2(b_conv_out), w_attn_out[l].astype(BF16), w_mix_out[l].astype(BF16), r2(g_post_mix))
            x1 = _mixer(xb, cv, gates, os_, stats, params)
            gate_half = jnp.asarray(np.where(np.arange(2 * FFN_HIDDEN) < FFN_HIDDEN, 0.5, 1.0), F32)
            w_ffn = (w_ffn_in[l] * gate_half).astype(BF16)
            xb = _ffn(x1, r2(g_pre_ffn), w_ffn, w_ffn_out[l].astype(BF16), r2(g_post_ffn))
        outs.append(xb)
    return jnp.stack(outs, axis=0)
```

```python
import functools
import math

import jax
import jax.numpy as jnp
import numpy as np
from jax import lax
from jax.experimental import pallas as pl
from jax.experimental.pallas import tpu as pltpu

D_MODEL = 1024
HEAD_DIM = 64
HEADS_PER_GROUP = 8
DILATED_GROUPS = ((128, 1), (512, 4), (2048, 16))
N_GROUPS = len(DILATED_GROUPS)
GROUP_W = HEADS_PER_GROUP * HEAD_DIM
ATTN_W = N_GROUPS * GROUP_W
QKV_W = 3 * GROUP_W
BLOCK = 128
REL_BUCKETS = 32
REL_MAX_DISTANCE = 2048
CONV_CHANNELS = D_MODEL
CONV_WIDTH = 31
FFN_HIDDEN = 2816
RMS_EPS = 1e-6
LN_EPS = 1e-5
NEG_INF = -1e30
LOG2E = math.log2(math.e)
GLU_COL = 3 * ATTN_W
GATE_COL = GLU_COL + 2 * CONV_CHANNELS

LANES = 128
SUBLANES = 8
CONV_ROWS = 64
LSE_W = LANES
HALO = 32
VMEM_LIMIT = 56 * 1024 * 1024

BF16 = jnp.bfloat16
F32 = jnp.float32


def _sigmoid_of_twice(h):
    return 0.5 * jnp.tanh(h) + 0.5


def _silu_of_twice(h):
    return h * (jnp.tanh(h) + 1.0)


def _rms(v, g):
    return v * lax.rsqrt(jnp.mean(v * v, axis=-1, keepdims=True) + RMS_EPS) * g


def _dot(a, b):
    return jnp.dot(a, b, preferred_element_type=F32)


def _resident(a):
    return pl.BlockSpec(a.shape, lambda *_: (0,) * a.ndim, pipeline_mode=pl.Buffered(1))


def _causal_dwconv_chunk(k, ext_ref, sh_ref, w_ref, bias, out_ref):
    tm = out_ref.shape[0]
    ns = sh_ref.shape[1]
    ext = ext_ref.at[k]
    lanes = slice(k * LANES, (k + 1) * LANES)
    for s in range(1, SUBLANES):
        sh_ref[s - 1] = ext[s:s + ns, :]
    for r0 in range(0, tm, CONV_ROWS):
        acc = jnp.zeros((CONV_ROWS, LANES), F32) + bias
        for j in range(CONV_WIDTH):
            a, s = divmod(HALO - (CONV_WIDTH - 1) + j, SUBLANES)
            src = ext if s == 0 else sh_ref.at[s - 1]
            row = a * SUBLANES + r0
            acc = acc + src[row:row + CONV_ROWS, :] * w_ref[k, j:j + 1, :]
        out_ref[r0:r0 + CONV_ROWS, lanes] = acc.astype(out_ref.dtype)


def _inproj_kernel(x_ref, g_ref, w_all_ref, b_glu_ref,
                   w_dw_ref, b_dw_ref, cv_ref, gate_ref, q0_ref, q1_ref, q2_ref,
                   h_ref, xn_ref, ext_ref, sh_ref):
    nx = D_MODEL // LANES
    tm = x_ref.shape[0]
    C = CONV_CHANNELS

    @pl.when(pl.program_id(0) == 0)
    def _():
        ext_ref[:, 0:HALO, :] = jnp.zeros((C // LANES, HALO, LANES), F32)

    xn = _rms(x_ref[...], g_ref[...])
    h_ref[0] = xn.astype(BF16)
    for k in range(nx):
        xn_ref[k] = xn[:, k * LANES:(k + 1) * LANES]
    for gi, (_, r) in enumerate(DILATED_GROUPS):
        if r == 1:
            continue
        n = tm // r
        for c in range(r):
            rows = jnp.concatenate([xn_ref[k, pl.ds(c, n, stride=r), :] for k in range(nx)], axis=-1)
            h_ref[gi, c * n:(c + 1) * n, :] = rows.astype(BF16)

    for k in range(C // LANES):
        val = slice(GLU_COL + k * LANES, GLU_COL + (k + 1) * LANES)
        gat = slice(GLU_COL + C + k * LANES, GLU_COL + C + (k + 1) * LANES)
        w_pair = jnp.concatenate([w_all_ref[:, val], w_all_ref[:, gat]], axis=1)
        z = _dot(h_ref[0], w_pair)
        zv = z[:, :LANES] + b_glu_ref[:, k * LANES:(k + 1) * LANES]
        zg = z[:, LANES:] + b_glu_ref[:, C + k * LANES:C + (k + 1) * LANES]
        ext_ref[k, HALO:HALO + tm, :] = zv * (jnp.tanh(zg) + 1.0)
        _causal_dwconv_chunk(k, ext_ref, sh_ref.at[k % 2], w_dw_ref, b_dw_ref[k], cv_ref)
    for k in range(C // LANES):
        ext_ref[k, 0:HALO, :] = ext_ref[k, tm:tm + HALO, :]

    gate_ref[...] = _sigmoid_of_twice(_dot(h_ref[0], w_all_ref[:, GATE_COL:])).astype(BF16)
    for gi, q_ref in enumerate((q0_ref, q1_ref, q2_ref)):
        for part in range(3):
            c0 = part * ATTN_W + gi * GROUP_W
            z = _dot(h_ref[gi], w_all_ref[:, c0:c0 + GROUP_W]).astype(BF16)
            q_ref[:, :, part * GROUP_W:(part + 1) * GROUP_W] = z.reshape(q_ref.shape[:2] + (GROUP_W,))


def _in_proj(x2, g, w_all, b_glu, w_dw, b_dw, tm=512):
    S = x2.shape[0]
    assert DILATED_GROUPS[0][1] == 1
    params = (g, w_all, b_glu, w_dw, b_dw)
    dil = [d for _, d in DILATED_GROUPS]
    return pl.pallas_call(
        _inproj_kernel,
        grid=(S // tm,),
        in_specs=[pl.BlockSpec((tm, D_MODEL), lambda i: (i, 0))] + [_resident(p) for p in params],
        out_specs=[pl.BlockSpec((tm, CONV_CHANNELS), lambda i: (i, 0)),
                   pl.BlockSpec((tm, 2 * D_MODEL), lambda i: (i, 0))]
                  + [pl.BlockSpec((r, tm // r, QKV_W), lambda i: (0, i, 0)) for r in dil],
        out_shape=[jax.ShapeDtypeStruct((S, CONV_CHANNELS), BF16),
                   jax.ShapeDtypeStruct((S, 2 * D_MODEL), BF16)]
                  + [jax.ShapeDtypeStruct((r, S // r, QKV_W), BF16) for r in dil],
        scratch_shapes=[pltpu.VMEM((N_GROUPS, tm, D_MODEL), BF16),
                        pltpu.VMEM((D_MODEL // LANES, tm, LANES), F32),
                        pltpu.VMEM((CONV_CHANNELS // LANES, HALO + tm, LANES), F32),
                        pltpu.VMEM((2, SUBLANES - 1, HALO + tm - SUBLANES, LANES), F32)],
        compiler_params=pltpu.CompilerParams(
            dimension_semantics=("arbitrary",), vmem_limit_bytes=VMEM_LIMIT),
        name="in_proj",
    )(x2, *params)


def _bucket_maps():
    max_exact = REL_BUCKETS // 2
    a = np.arange(BLOCK, dtype=np.int64)[:, None]
    c = np.arange(2 * BLOCK, dtype=np.int64)[None, :]
    offset = a - c + BLOCK
    maps = []
    for window, dilation in DILATED_GROUPS:
        span = window // dilation
        assert span == BLOCK
        d = np.maximum(offset * dilation, 0)
        df = np.maximum(d, 1).astype(np.float32)
        large = max_exact + (np.log(df / np.float32(max_exact)) / np.float32(math.log(REL_MAX_DISTANCE / max_exact))
                             * np.float32(REL_BUCKETS - max_exact)).astype(np.int32)
        large = np.minimum(large, REL_BUCKETS - 1)
        bucket = np.where(d < max_exact, d, large)
        valid = (offset >= 0) & (offset <= span)
        maps.append(np.where(valid, bucket, -1).astype(np.int32))
    return np.stack(maps)


def _bias_kernel(tab_ref, bucket_ref, o_ref):
    g = pl.program_id(0)
    bucket = bucket_ref[...]
    col = lax.broadcasted_iota(jnp.int32, bucket.shape, 1)
    in_bucket = [bucket == b for b in range(REL_BUCKETS)]
    for h in range(HEADS_PER_GROUP):
        acc = jnp.full(bucket.shape, NEG_INF, F32)
        for b in range(REL_BUCKETS):
            acc = jnp.where(in_bucket[b], tab_ref[b, g * HEADS_PER_GROUP + h] * LOG2E, acc)
        o_ref[0, h] = acc
        o_ref[1, h] = jnp.where(col < BLOCK, NEG_INF, acc)


def _bias_tables(rel_bias_table):
    return pl.pallas_call(
        _bias_kernel,
        grid=(N_GROUPS,),
        in_specs=[
            pl.BlockSpec(memory_space=pltpu.SMEM),
            pl.BlockSpec((None, BLOCK, 2 * BLOCK), lambda g: (g, 0, 0)),
        ],
        out_specs=pl.BlockSpec((None, 2, HEADS_PER_GROUP, BLOCK, 2 * BLOCK), lambda g: (g, 0, 0, 0, 0)),
        out_shape=jax.ShapeDtypeStruct((N_GROUPS, 2, HEADS_PER_GROUP, BLOCK, 2 * BLOCK), F32),
        name="bias_tables",
    )(rel_bias_table.astype(F32), jnp.asarray(_bucket_maps()))


def _attn_kernel(q_ref, kp_ref, kc_ref, vp_ref, vc_ref, bias_ref, sumcol_ref, o_ref, ml_ref):
    first = jnp.where(pl.program_id(1) == 0, 1, 0)
    lane = lax.broadcasted_iota(jnp.int32, (BLOCK, LANES), 1)
    low = lane < HEAD_DIM
    for j in range(q_ref.shape[0] // BLOCK):
        rows = slice(j * BLOCK, (j + 1) * BLOCK)
        variant = first if j == 0 else 0
        ml = jnp.zeros((BLOCK, LANES), F32)
        for pair in range(HEADS_PER_GROUP // 2):
            sl = slice(pair * LANES, (pair + 1) * LANES)
            q = q_ref[rows, sl] * jnp.asarray(HEAD_DIM ** -0.5, BF16)
            if j == 0:
                k = jnp.concatenate([kp_ref[:, sl], kc_ref[rows, sl]], axis=0)
                v = jnp.concatenate([vp_ref[:, sl], vc_ref[rows, sl]], axis=0)
            else:
                k = kc_ref[(j - 1) * BLOCK:(j + 1) * BLOCK, sl]
                v = vc_ref[(j - 1) * BLOCK:(j + 1) * BLOCK, sl]
            halves = []
            for sub in range(2):
                h = 2 * pair + sub
                qm = jnp.where(low if sub == 0 else ~low, q, jnp.zeros_like(q))
                s = lax.dot_general(qm, k, (((1,), (1,)), ((), ())), preferred_element_type=F32)
                t = s * LOG2E + bias_ref[variant, h]
                mt = jnp.max(t, axis=-1, keepdims=True)
                p = jnp.exp2(t - mt).astype(BF16)
                r = _dot(p, jnp.concatenate([v, sumcol_ref[h]], axis=1))
                halves.append(r[:, :LANES])
                ml = jnp.where(lane == h, mt, ml) + r[:, LANES:]
            o_ref[rows, sl] = jnp.where(low, halves[0], halves[1]).astype(o_ref.dtype)
        ml_ref[rows, :] = ml


def _attention_group(qkv, biases, g, nq=8):
    r, L, _ = qkv.shape
    tq = nq * BLOCK
    sumcol = jnp.asarray(np.broadcast_to(
        np.arange(LANES)[None, None, :] == HEADS_PER_GROUP + np.arange(HEADS_PER_GROUP)[:, None, None],
        (HEADS_PER_GROUP, 2 * BLOCK, LANES)), dtype=BF16)

    def spec(part, prev):
        if prev:
            return pl.BlockSpec((None, BLOCK, GROUP_W), lambda c, i: (c, jnp.maximum(i * nq - 1, 0), part))
        return pl.BlockSpec((None, tq, GROUP_W), lambda c, i: (c, i, part))

    return pl.pallas_call(
        _attn_kernel,
        grid=(r, L // tq),
        in_specs=[
            spec(0, False),
            spec(1, True), spec(1, False),
            spec(2, True), spec(2, False),
            pl.BlockSpec((None, 2, HEADS_PER_GROUP, BLOCK, 2 * BLOCK), lambda c, i: (g, 0, 0, 0, 0)),
            pl.BlockSpec(sumcol.shape, lambda c, i: (0, 0, 0)),
        ],
        out_specs=[
            pl.BlockSpec((None, tq, GROUP_W), lambda c, i: (c, i, 0)),
            pl.BlockSpec((None, tq, LSE_W), lambda c, i: (c, i, 0)),
        ],
        out_shape=[
            jax.ShapeDtypeStruct((r, L, GROUP_W), BF16),
            jax.ShapeDtypeStruct((r, L, LSE_W), F32),
        ],
        compiler_params=pltpu.CompilerParams(
            dimension_semantics=("parallel", "arbitrary"), vmem_limit_bytes=VMEM_LIMIT),
        name=f"attn_g{g}",
    )(qkv, qkv, qkv, qkv, qkv, biases, sumcol)


def _mixer_kernel(o0_ref, o1_ref, o2_ref, l0_ref, l1_ref, l2_ref, cv_ref, ga_ref, gc_ref, x_ref,
                  expand_ref, g_ln_ref, b_ln_ref, w_co_ref, b_co_ref, w_ao_ref, w_mo_ref, g_pm_ref,
                  out_ref, on_ref, ln_ref):
    def natural(src_ref, dst_ref):
        r, n, w = src_ref.shape
        if r == 1:
            return src_ref[0].astype(F32)
        for c in range(r):
            for k in range(w // LANES):
                dst_ref[k, pl.ds(c, n, stride=r), :] = src_ref[c, :, k * LANES:(k + 1) * LANES].astype(F32)
        return jnp.concatenate([dst_ref[k] for k in range(w // LANES)], axis=-1)

    ml = [natural(r_, ln_ref.at[g]) for g, r_ in enumerate((l0_ref, l1_ref, l2_ref))]
    den = [pltpu.roll(a, LANES - HEADS_PER_GROUP, axis=1) for a in ml]
    mx = jnp.maximum(jnp.maximum(ml[0], ml[1]), ml[2])
    w = [jnp.exp2(a - mx) for a in ml]
    total = w[0] * den[0] + w[1] * den[1] + w[2] * den[2]
    head_lane = lax.broadcasted_iota(jnp.int32, total.shape, 1) < HEADS_PER_GROUP
    inv = jnp.where(head_lane, 1.0 / total, 0.0)
    expand = expand_ref[...]

    def widen(a):
        hi = a.astype(BF16)
        lo = (a - hi.astype(F32)).astype(BF16)
        return _dot(jnp.concatenate([hi, lo], axis=1), expand)

    o = (widen(w[0] * inv) * natural(o0_ref, on_ref.at[0])
         + widen(w[1] * inv) * natural(o1_ref, on_ref.at[1])
         + widen(w[2] * inv) * natural(o2_ref, on_ref.at[2]))
    y_attn = _dot(o.astype(BF16), w_ao_ref[...])

    acc = cv_ref[...].astype(F32)
    mu = jnp.mean(acc, axis=-1, keepdims=True)
    xc = acc - mu
    y = xc * lax.rsqrt(jnp.mean(xc * xc, axis=-1, keepdims=True) + LN_EPS)
    y = _silu_of_twice(y * g_ln_ref[...] + b_ln_ref[...])
    y_conv = _dot(y.astype(BF16), w_co_ref[...]) + b_co_ref[...]

    merged = ga_ref[...].astype(F32) * y_attn + gc_ref[...].astype(F32) * y_conv
    mix = _dot(merged.astype(BF16), w_mo_ref[...])
    out_ref[...] = x_ref[...] + _rms(mix, g_pm_ref[...])


def _mixer(x2, cv, gates, os_, stats, params, tm=512):
    S = x2.shape[0]

    def row(w, col=0):
        return pl.BlockSpec((tm, w), lambda i: (i, col))

    def rspec(a):
        r, _, w = a.shape
        return pl.BlockSpec((r, tm // r, w), lambda i: (0, i, 0))

    return pl.pallas_call(
        _mixer_kernel,
        grid=(S // tm,),
        in_specs=[rspec(a) for a in os_] + [rspec(a) for a in stats]
                 + [row(CONV_CHANNELS), row(D_MODEL, 0), row(D_MODEL, 1), row(D_MODEL)]
                 + [_resident(p) for p in params],
        out_specs=row(D_MODEL),
        out_shape=jax.ShapeDtypeStruct((S, D_MODEL), F32),
        scratch_shapes=[pltpu.VMEM((N_GROUPS, GROUP_W // LANES, tm, LANES), F32),
                        pltpu.VMEM((N_GROUPS, LSE_W // LANES, tm, LANES), F32)],
        compiler_params=pltpu.CompilerParams(
            dimension_semantics=("parallel",), vmem_limit_bytes=VMEM_LIMIT),
        name="mixer",
    )(*os_, *stats, cv, gates, gates, x2, *params)


def _ffn_kernel(x_ref, g_pre_ref, w_in_ref, w_out_ref, g_post_ref, out_ref, *, chunk):
    x = x_ref[...]
    h = _rms(x, g_pre_ref[...]).astype(BF16)
    acc = jnp.zeros(x.shape, F32)
    for c0 in range(0, FFN_HIDDEN, chunk):
        half_gate = _dot(h, w_in_ref[:, c0:c0 + chunk])
        up = _dot(h, w_in_ref[:, FFN_HIDDEN + c0:FFN_HIDDEN + c0 + chunk])
        act = (_silu_of_twice(half_gate) * up).astype(BF16)
        acc = acc + _dot(act, w_out_ref[c0:c0 + chunk, :])
    out_ref[...] = x + _rms(acc, g_post_ref[...])


def _ffn(x1, g_pre, w_in, w_out, g_post, tm=512, chunk=1408):
    S = x1.shape[0]
    row = pl.BlockSpec((tm, D_MODEL), lambda i: (i, 0))
    params = (g_pre, w_in, w_out, g_post)
    return pl.pallas_call(
        functools.partial(_ffn_kernel, chunk=chunk),
        grid=(S // tm,),
        in_specs=[row] + [_resident(p) for p in params],
        out_specs=row,
        out_shape=jax.ShapeDtypeStruct((S, D_MODEL), F32),
        compiler_params=pltpu.CompilerParams(
            dimension_semantics=("parallel",), vmem_limit_bytes=VMEM_LIMIT),
        name="ffn",
    )(x1, *params)


def _lane_chunks(a):
    return a.reshape(a.shape[0], -1, LANES).transpose(1, 0, 2)


def _halved_cols(w, lo, hi):
    col = np.arange(w.shape[1])
    scale = jnp.asarray(np.where((col >= lo) & (col < hi), 0.5, 1.0), F32)
    return (w * scale).astype(BF16)


def kernel(x, rel_bias_table, g_pre_mix, w_in, b_glu, w_dw, b_dw, g_conv_ln, b_conv_ln, w_conv_out, b_conv_out, w_attn_out, w_mix_out, g_post_mix, g_pre_ffn, w_ffn_in, w_ffn_out, g_post_ffn):
    B, S, D = x.shape
    depth = w_in.shape[0]
    expand = jnp.asarray(
        (np.arange(2 * LSE_W)[:, None] % LSE_W == np.arange(GROUP_W)[None, :] // HEAD_DIM), dtype=BF16)
    biases = _bias_tables(rel_bias_table)
    outs = []
    for b in range(B):
        xb = x[b]
        for l in range(depth):
            r2 = lambda a: a[l].reshape(1, -1)
            w_all = _halved_cols(w_in[l], GLU_COL, w_in.shape[2])
            cv, gates, *qkvs = _in_proj(xb, r2(g_pre_mix), w_all, 0.5 * r2(b_glu),
                                        _lane_chunks(w_dw[l]), _lane_chunks(r2(b_dw)))
            os_, stats = [], []
            for g, qkv in enumerate(qkvs):
                o_g, ml_g = _attention_group(qkv, biases, g)
                os_.append(o_g)
                stats.append(ml_g)
            params = (expand, 0.5 * r2(g_conv_ln), 0.5 * r2(b_conv_ln), w_conv_out[l].astype(BF16),
                      r2(b_conv_out), w_attn_out[l].astype(BF16), w_mix_out[l].astype(BF16), r2(g_post_mix))
            x1 = _mixer(xb, cv, gates, os_, stats, params)
            xb = _ffn(x1, r2(g_pre_ffn), _halved_cols(w_ffn_in[l], 0, FFN_HIDDEN), w_ffn_out[l].astype(BF16),
                      r2(g_post_ffn))
        outs.append(xb)
    return jnp.stack(outs, axis=0)
```

```python
import functools
import math

import jax
import jax.numpy as jnp
import numpy as np
from jax import lax
from jax.experimental import pallas as pl
from jax.experimental.pallas import tpu as pltpu

D_MODEL = 1024
HEAD_DIM = 64
HEADS_PER_GROUP = 8
DILATED_GROUPS = ((128, 1), (512, 4), (2048, 16))
N_GROUPS = len(DILATED_GROUPS)
GROUP_W = HEADS_PER_GROUP * HEAD_DIM
ATTN_W = N_GROUPS * GROUP_W
QKV_W = 3 * GROUP_W
BLOCK = 128
REL_BUCKETS = 32
REL_MAX_DISTANCE = 2048
CONV_CHANNELS = D_MODEL
CONV_WIDTH = 31
FFN_HIDDEN = 2816
RMS_EPS = 1e-6
LN_EPS = 1e-5
NEG_INF = -1e30
LOG2E = math.log2(math.e)
GLU_COL = 3 * ATTN_W
GATE_COL = GLU_COL + 2 * CONV_CHANNELS

LANES = 128
SUBLANES = 8
CONV_ROWS = 64
LSE_W = LANES
HALO = 32
VMEM_LIMIT = 56 * 1024 * 1024

BF16 = jnp.bfloat16
F32 = jnp.float32


def _sigmoid_of_twice(h):
    return 0.5 * jnp.tanh(h) + 0.5


def _silu_of_twice(h):
    return h * (jnp.tanh(h) + 1.0)


def _rms(v, g):
    return v * lax.rsqrt(jnp.mean(v * v, axis=-1, keepdims=True) + RMS_EPS) * g


def _dot(a, b):
    return jnp.dot(a, b, preferred_element_type=F32)


def _resident(a):
    return pl.BlockSpec(a.shape, lambda *_: (0,) * a.ndim, pipeline_mode=pl.Buffered(1))


def _causal_dwconv_chunk(k, ext_ref, sh_ref, w_ref, bias, out_ref):
    tm = out_ref.shape[0]
    ns = sh_ref.shape[1]
    ext = ext_ref.at[k]
    lanes = slice(k * LANES, (k + 1) * LANES)
    for s in range(1, SUBLANES):
        sh_ref[s - 1] = ext[s:s + ns, :]
    for r0 in range(0, tm, CONV_ROWS):
        acc = jnp.zeros((CONV_ROWS, LANES), F32) + bias
        for j in range(CONV_WIDTH):
            a, s = divmod(HALO - (CONV_WIDTH - 1) + j, SUBLANES)
            src = ext if s == 0 else sh_ref.at[s - 1]
            row = a * SUBLANES + r0
            acc = acc + src[row:row + CONV_ROWS, :] * w_ref[k, j:j + 1, :]
        out_ref[r0:r0 + CONV_ROWS, lanes] = acc.astype(out_ref.dtype)


def _inproj_kernel(x_ref, g_ref, w_all_ref, b_glu_ref,
                   w_dw_ref, b_dw_ref, cv_ref, gate_ref, q0_ref, q1_ref, q2_ref,
                   h_ref, xn_ref, ext_ref, sh_ref):
    nx = D_MODEL // LANES
    tm = x_ref.shape[0]
    C = CONV_CHANNELS

    @pl.when(pl.program_id(0) == 0)
    def _():
        ext_ref[:, 0:HALO, :] = jnp.zeros((C // LANES, HALO, LANES), F32)

    xn = _rms(x_ref[...], g_ref[...])
    h_ref[0] = xn.astype(BF16)
    for k in range(nx):
        xn_ref[k] = xn[:, k * LANES:(k + 1) * LANES]
    for gi, (_, r) in enumerate(DILATED_GROUPS):
        if r == 1:
            continue
        n = tm // r
        for c in range(r):
            rows = jnp.concatenate([xn_ref[k, pl.ds(c, n, stride=r), :] for k in range(nx)], axis=-1)
            h_ref[gi, c * n:(c + 1) * n, :] = rows.astype(BF16)

    for k in range(C // LANES):
        val = slice(GLU_COL + k * LANES, GLU_COL + (k + 1) * LANES)
        gat = slice(GLU_COL + C + k * LANES, GLU_COL + C + (k + 1) * LANES)
        w_pair = jnp.concatenate([w_all_ref[:, val], w_all_ref[:, gat]], axis=1)
        z = _dot(h_ref[0], w_pair)
        zv = z[:, :LANES] + b_glu_ref[:, k * LANES:(k + 1) * LANES]
        zg = z[:, LANES:] + b_glu_ref[:, C + k * LANES:C + (k + 1) * LANES]
        ext_ref[k, HALO:HALO + tm, :] = zv * (jnp.tanh(zg) + 1.0)
        _causal_dwconv_chunk(k, ext_ref, sh_ref.at[k % 2], w_dw_ref, b_dw_ref[k], cv_ref)
    for k in range(C // LANES):
        ext_ref[k, 0:HALO, :] = ext_ref[k, tm:tm + HALO, :]

    gate_ref[...] = _sigmoid_of_twice(_dot(h_ref[0], w_all_ref[:, GATE_COL:])).astype(BF16)
    for gi, q_ref in enumerate((q0_ref, q1_ref, q2_ref)):
        for part in range(3):
            c0 = part * ATTN_W + gi * GROUP_W
            z = _dot(h_ref[gi], w_all_ref[:, c0:c0 + GROUP_W]).astype(BF16)
            q_ref[:, :, part * GROUP_W:(part + 1) * GROUP_W] = z.reshape(q_ref.shape[:2] + (GROUP_W,))


def _in_proj(x2, g, w_all, b_glu, w_dw, b_dw, tm=512):
    S = x2.shape[0]
    assert DILATED_GROUPS[0][1] == 1
    params = (g, w_all, b_glu, w_dw, b_dw)
    dil = [d for _, d in DILATED_GROUPS]
    return pl.pallas_call(
        _inproj_kernel,
        grid=(S // tm,),
        in_specs=[pl.BlockSpec((tm, D_MODEL), lambda i: (i, 0))] + [_resident(p) for p in params],
        out_specs=[pl.BlockSpec((tm, CONV_CHANNELS), lambda i: (i, 0)),
                   pl.BlockSpec((tm, 2 * D_MODEL), lambda i: (i, 0))]
                  + [pl.BlockSpec((r, tm // r, QKV_W), lambda i: (0, i, 0)) for r in dil],
        out_shape=[jax.ShapeDtypeStruct((S, CONV_CHANNELS), BF16),
                   jax.ShapeDtypeStruct((S, 2 * D_MODEL), BF16)]
                  + [jax.ShapeDtypeStruct((r, S // r, QKV_W), BF16) for r in dil],
        scratch_shapes=[pltpu.VMEM((N_GROUPS, tm, D_MODEL), BF16),
                        pltpu.VMEM((D_MODEL // LANES, tm, LANES), F32),
                        pltpu.VMEM((CONV_CHANNELS // LANES, HALO + tm, LANES), F32),
                        pltpu.VMEM((2, SUBLANES - 1, HALO + tm - SUBLANES, LANES), F32)],
        compiler_params=pltpu.CompilerParams(
            dimension_semantics=("arbitrary",), vmem_limit_bytes=VMEM_LIMIT),
        name="in_proj",
    )(x2, *params)


def _bucket_maps():
    max_exact = REL_BUCKETS // 2
    a = np.arange(BLOCK, dtype=np.int64)[:, None]
    c = np.arange(2 * BLOCK, dtype=np.int64)[None, :]
    offset = a - c + BLOCK
    maps = []
    for window, dilation in DILATED_GROUPS:
        span = window // dilation
        assert span == BLOCK
        d = np.maximum(offset * dilation, 0)
        df = np.maximum(d, 1).astype(np.float32)
        large = max_exact + (np.log(df / np.float32(max_exact)) / np.float32(math.log(REL_MAX_DISTANCE / max_exact))
                             * np.float32(REL_BUCKETS - max_exact)).astype(np.int32)
        large = np.minimum(large, REL_BUCKETS - 1)
        bucket = np.where(d < max_exact, d, large)
        valid = (offset >= 0) & (offset <= span)
        maps.append(np.where(valid, bucket, -1).astype(np.int32))
    return np.stack(maps)


def _bias_kernel(tab_ref, bucket_ref, o_ref):
    g = pl.program_id(0)
    bucket = bucket_ref[...]
    col = lax.broadcasted_iota(jnp.int32, bucket.shape, 1)
    in_bucket = [bucket == b for b in range(REL_BUCKETS)]
    for h in range(HEADS_PER_GROUP):
        acc = jnp.full(bucket.shape, NEG_INF, F32)
        for b in range(REL_BUCKETS):
            acc = jnp.where(in_bucket[b], tab_ref[b, g * HEADS_PER_GROUP + h] * LOG2E, acc)
        o_ref[0, h] = acc
        o_ref[1, h] = jnp.where(col < BLOCK, NEG_INF, acc)


def _bias_tables(rel_bias_table):
    return pl.pallas_call(
        _bias_kernel,
        grid=(N_GROUPS,),
        in_specs=[
            pl.BlockSpec(memory_space=pltpu.SMEM),
            pl.BlockSpec((None, BLOCK, 2 * BLOCK), lambda g: (g, 0, 0)),
        ],
        out_specs=pl.BlockSpec((None, 2, HEADS_PER_GROUP, BLOCK, 2 * BLOCK), lambda g: (g, 0, 0, 0, 0)),
        out_shape=jax.ShapeDtypeStruct((N_GROUPS, 2, HEADS_PER_GROUP, BLOCK, 2 * BLOCK), F32),
        name="bias_tables",
    )(rel_bias_table.astype(F32), jnp.asarray(_bucket_maps()))


def _attn_kernel(q_ref, kp_ref, kc_ref, vp_ref, vc_ref, bias_ref, sumcol_ref, o_ref, ml_ref):
    first = jnp.where(pl.program_id(1) == 0, 1, 0)
    lane = lax.broadcasted_iota(jnp.int32, (BLOCK, LANES), 1)
    low = lane < HEAD_DIM
    for j in range(q_ref.shape[0] // BLOCK):
        rows = slice(j * BLOCK, (j + 1) * BLOCK)
        variant = first if j == 0 else 0
        ml = jnp.zeros((BLOCK, LANES), F32)
        for pair in range(HEADS_PER_GROUP // 2):
            sl = slice(pair * LANES, (pair + 1) * LANES)
            q = q_ref[rows, sl] * jnp.asarray(HEAD_DIM ** -0.5, BF16)
            if j == 0:
                k = jnp.concatenate([kp_ref[:, sl], kc_ref[rows, sl]], axis=0)
                v = jnp.concatenate([vp_ref[:, sl], vc_ref[rows, sl]], axis=0)
            else:
                k = kc_ref[(j - 1) * BLOCK:(j + 1) * BLOCK, sl]
                v = vc_ref[(j - 1) * BLOCK:(j + 1) * BLOCK, sl]
            halves = []
            for sub in range(2):
                h = 2 * pair + sub
                qm = jnp.where(low if sub == 0 else ~low, q, jnp.zeros_like(q))
                s = lax.dot_general(qm, k, (((1,), (1,)), ((), ())), preferred_element_type=F32)
                t = s * LOG2E + bias_ref[variant, h]
                mt = jnp.max(t, axis=-1, keepdims=True)
                p = jnp.exp2(t - mt).astype(BF16)
                r = _dot(p, jnp.concatenate([v, sumcol_ref[h]], axis=1))
                halves.append(r[:, :LANES])
                ml = jnp.where(lane == h, mt, ml) + r[:, LANES:]
            o_ref[rows, sl] = jnp.where(low, halves[0], halves[1]).astype(o_ref.dtype)
        ml_ref[rows, :] = ml


def _attention_group(qkv, biases, g, nq=8):
    r, L, _ = qkv.shape
    tq = nq * BLOCK
    sumcol = jnp.asarray(np.broadcast_to(
        np.arange(LANES)[None, None, :] == HEADS_PER_GROUP + np.arange(HEADS_PER_GROUP)[:, None, None],
        (HEADS_PER_GROUP, 2 * BLOCK, LANES)), dtype=BF16)

    def spec(part, prev):
        if prev:
            return pl.BlockSpec((None, BLOCK, GROUP_W), lambda c, i: (c, jnp.maximum(i * nq - 1, 0), part))
        return pl.BlockSpec((None, tq, GROUP_W), lambda c, i: (c, i, part))

    return pl.pallas_call(
        _attn_kernel,
        grid=(r, L // tq),
        in_specs=[
            spec(0, False),
            spec(1, True), spec(1, False),
            spec(2, True), spec(2, False),
            pl.BlockSpec((None, 2, HEADS_PER_GROUP, BLOCK, 2 * BLOCK), lambda c, i: (g, 0, 0, 0, 0)),
            pl.BlockSpec(sumcol.shape, lambda c, i: (0, 0, 0)),
        ],
        out_specs=[
            pl.BlockSpec((None, tq, GROUP_W), lambda c, i: (c, i, 0)),
            pl.BlockSpec((None, tq, LSE_W), lambda c, i: (c, i, 0)),
        ],
        out_shape=[
            jax.ShapeDtypeStruct((r, L, GROUP_W), BF16),
            jax.ShapeDtypeStruct((r, L, LSE_W), F32),
        ],
        compiler_params=pltpu.CompilerParams(
            dimension_semantics=("parallel", "arbitrary"), vmem_limit_bytes=VMEM_LIMIT),
        name=f"attn_g{g}",
    )(qkv, qkv, qkv, qkv, qkv, biases, sumcol)


def _mixer_kernel(o0_ref, o1_ref, o2_ref, l0_ref, l1_ref, l2_ref, cv_ref, ga_ref, gc_ref, x_ref,
                  expand_ref, g_ln_ref, b_ln_ref, w_co_ref, b_co_ref, w_ao_ref, w_mo_ref, g_pm_ref,
                  out_ref, on_ref, ln_ref):
    def natural(src_ref, dst_ref):
        r, n, w = src_ref.shape
        if r == 1:
            return src_ref[0].astype(F32)
        for c in range(r):
            for k in range(w // LANES):
                dst_ref[k, pl.ds(c, n, stride=r), :] = src_ref[c, :, k * LANES:(k + 1) * LANES].astype(F32)
        return jnp.concatenate([dst_ref[k] for k in range(w // LANES)], axis=-1)

    ml = [natural(r_, ln_ref.at[g]) for g, r_ in enumerate((l0_ref, l1_ref, l2_ref))]
    den = [pltpu.roll(a, LANES - HEADS_PER_GROUP, axis=1) for a in ml]
    mx = jnp.maximum(jnp.maximum(ml[0], ml[1]), ml[2])
    w = [jnp.exp2(a - mx) for a in ml]
    total = w[0] * den[0] + w[1] * den[1] + w[2] * den[2]
    head_lane = lax.broadcasted_iota(jnp.int32, total.shape, 1) < HEADS_PER_GROUP
    inv = jnp.where(head_lane, 1.0 / total, 0.0)
    expand = expand_ref[...]

    def widen(a):
        hi = a.astype(BF16)
        lo = (a - hi.astype(F32)).astype(BF16)
        return _dot(jnp.concatenate([hi, lo], axis=1), expand)

    o = (widen(w[0] * inv) * natural(o0_ref, on_ref.at[0])
         + widen(w[1] * inv) * natural(o1_ref, on_ref.at[1])
         + widen(w[2] * inv) * natural(o2_ref, on_ref.at[2]))
    y_attn = _dot(o.astype(BF16), w_ao_ref[...])

    acc = cv_ref[...].astype(F32)
    mu = jnp.mean(acc, axis=-1, keepdims=True)
    xc = acc - mu
    y = xc * lax.rsqrt(jnp.mean(xc * xc, axis=-1, keepdims=True) + LN_EPS)
    y = _silu_of_twice(y * g_ln_ref[...] + b_ln_ref[...])
    y_conv = _dot(y.astype(BF16), w_co_ref[...]) + b_co_ref[...]

    merged = ga_ref[...].astype(F32) * y_attn + gc_ref[...].astype(F32) * y_conv
    mix = _dot(merged.astype(BF16), w_mo_ref[...])
    out_ref[...] = x_ref[...] + _rms(mix, g_pm_ref[...])


def _mixer(x2, cv, gates, os_, stats, params, tm=512):
    S = x2.shape[0]

    def row(w, col=0):
        return pl.BlockSpec((tm, w), lambda i: (i, col))

    def rspec(a):
        r, _, w = a.shape
        return pl.BlockSpec((r, tm // r, w), lambda i: (0, i, 0))

    return pl.pallas_call(
        _mixer_kernel,
        grid=(S // tm,),
        in_specs=[rspec(a) for a in os_] + [rspec(a) for a in stats]
                 + [row(CONV_CHANNELS), row(D_MODEL, 0), row(D_MODEL, 1), row(D_MODEL)]
                 + [_resident(p) for p in params],
        out_specs=row(D_MODEL),
        out_shape=jax.ShapeDtypeStruct((S, D_MODEL), F32),
        scratch_shapes=[pltpu.VMEM((N_GROUPS, GROUP_W // LANES, tm, LANES), F32),
                        pltpu.VMEM((N_GROUPS, LSE_W // LANES, tm, LANES), F32)],
        compiler_params=pltpu.CompilerParams(
            dimension_semantics=("parallel",), vmem_limit_bytes=VMEM_LIMIT),
        name="mixer",
    )(*os_, *stats, cv, gates, gates, x2, *params)


def _ffn_kernel(x_ref, g_pre_ref, w_in_ref, w_out_ref, g_post_ref, out_ref, h_ref, act_ref, *, chunk):
    x = x_ref[...]
    h_ref[...] = _rms(x, g_pre_ref[...]).astype(BF16)
    for c0 in range(0, FFN_HIDDEN, chunk):
        half_gate = _dot(h_ref[...], w_in_ref[:, c0:c0 + chunk])
        up = _dot(h_ref[...], w_in_ref[:, FFN_HIDDEN + c0:FFN_HIDDEN + c0 + chunk])
        act_ref[:, c0:c0 + chunk] = (_silu_of_twice(half_gate) * up).astype(BF16)
    out_ref[...] = x + _rms(_dot(act_ref[...], w_out_ref[...]), g_post_ref[...])


def _ffn(x1, g_pre, w_in, w_out, g_post, tm=512, chunk=256):
    S = x1.shape[0]
    row = pl.BlockSpec((tm, D_MODEL), lambda i: (i, 0))
    params = (g_pre, w_in, w_out, g_post)
    return pl.pallas_call(
        functools.partial(_ffn_kernel, chunk=chunk),
        grid=(S // tm,),
        in_specs=[row] + [_resident(p) for p in params],
        out_specs=row,
        out_shape=jax.ShapeDtypeStruct((S, D_MODEL), F32),
        scratch_shapes=[pltpu.VMEM((tm, D_MODEL), BF16), pltpu.VMEM((tm, FFN_HIDDEN), BF16)],
        compiler_params=pltpu.CompilerParams(
            dimension_semantics=("parallel",), vmem_limit_bytes=VMEM_LIMIT),
        name="ffn",
    )(x1, *params)


def _lane_chunks(a):
    return a.reshape(a.shape[0], -1, LANES).transpose(1, 0, 2)


def _halved_cols(w, lo, hi):
    col = np.arange(w.shape[1])
    scale = jnp.asarray(np.where((col >= lo) & (col < hi), 0.5, 1.0), F32)
    return (w * scale).astype(BF16)


def kernel(x, rel_bias_table, g_pre_mix, w_in, b_glu, w_dw, b_dw, g_conv_ln, b_conv_ln, w_conv_out, b_conv_out, w_attn_out, w_mix_out, g_post_mix, g_pre_ffn, w_ffn_in, w_ffn_out, g_post_ffn):
    B, S, D = x.shape
    depth = w_in.shape[0]
    expand = jnp.asarray(
        (np.arange(2 * LSE_W)[:, None] % LSE_W == np.arange(GROUP_W)[None, :] // HEAD_DIM), dtype=BF16)
    biases = _bias_tables(rel_bias_table)
    outs = []
    for b in range(B):
        xb = x[b]
        for l in range(depth):
            r2 = lambda a: a[l].reshape(1, -1)
            w_all = _halved_cols(w_in[l], GLU_COL, w_in.shape[2])
            cv, gates, *qkvs = _in_proj(xb, r2(g_pre_mix), w_all, 0.5 * r2(b_glu),
                                        _lane_chunks(w_dw[l]), _lane_chunks(r2(b_dw)))
            os_, stats = [], []
            for g, qkv in enumerate(qkvs):
                o_g, ml_g = _attention_group(qkv, biases, g)
                os_.append(o_g)
                stats.append(ml_g)
            params = (expand, 0.5 * r2(g_conv_ln), 0.5 * r2(b_conv_ln), w_conv_out[l].astype(BF16),
                      r2(b_conv_out), w_attn_out[l].astype(BF16), w_mix_out[l].astype(BF16), r2(g_post_mix))
            x1 = _mixer(xb, cv, gates, os_, stats, params)
            xb = _ffn(x1, r2(g_pre_ffn), _halved_cols(w_ffn_in[l], 0, FFN_HIDDEN), w_ffn_out[l].astype(BF16),
                      r2(g_post_ffn))
        outs.append(xb)
    return jnp.stack(outs, axis=0)
```

```python
import functools
import math

import jax
import jax.numpy as jnp
import numpy as np
from jax import lax
from jax.experimental import pallas as pl
from jax.experimental.pallas import tpu as pltpu

D_MODEL = 1024
HEAD_DIM = 64
HEADS_PER_GROUP = 8
DILATED_GROUPS = ((128, 1), (512, 4), (2048, 16))
N_GROUPS = len(DILATED_GROUPS)
GROUP_W = HEADS_PER_GROUP * HEAD_DIM
ATTN_W = N_GROUPS * GROUP_W
QKV_W = 3 * GROUP_W
BLOCK = 128
REL_BUCKETS = 32
REL_MAX_DISTANCE = 2048
CONV_CHANNELS = D_MODEL
CONV_WIDTH = 31
FFN_HIDDEN = 2816
RMS_EPS = 1e-6
LN_EPS = 1e-5
NEG_INF = -1e30
LOG2E = math.log2(math.e)
QK_SCALE = HEAD_DIM ** -0.5 * LOG2E
GLU_COL = 3 * ATTN_W
GATE_COL = GLU_COL + 2 * CONV_CHANNELS

LANES = 128
SUBLANES = 8
CONV_ROWS = 64
LSE_W = LANES
HALO = 32
GATE_PIECE = 256
VMEM_LIMIT = 56 * 1024 * 1024

BF16 = jnp.bfloat16
F32 = jnp.float32


def _sigmoid_of_twice(h):
    return 0.5 * jnp.tanh(h) + 0.5


def _silu_of_twice(h):
    return h * (jnp.tanh(h) + 1.0)


def _rms(v, g):
    return v * lax.rsqrt(jnp.mean(v * v, axis=-1, keepdims=True) + RMS_EPS) * g


def _dot(a, b):
    return jnp.dot(a, b, preferred_element_type=F32)


def _resident(a):
    return pl.BlockSpec(a.shape, lambda *_: (0,) * a.ndim, pipeline_mode=pl.Buffered(1))


def _causal_dwconv_chunk(k, ext_ref, sh_ref, w_ref, bias, out_ref):
    tm = out_ref.shape[0]
    ns = sh_ref.shape[1]
    ext = ext_ref.at[k]
    lanes = slice(k * LANES, (k + 1) * LANES)
    for s in range(1, SUBLANES):
        sh_ref[s - 1] = ext[s:s + ns, :]
    for r0 in range(0, tm, CONV_ROWS):
        acc = jnp.zeros((CONV_ROWS, LANES), F32) + bias
        for j in range(CONV_WIDTH):
            a, s = divmod(HALO - (CONV_WIDTH - 1) + j, SUBLANES)
            src = ext if s == 0 else sh_ref.at[s - 1]
            row = a * SUBLANES + r0
            acc = acc + src[row:row + CONV_ROWS, :] * w_ref[k, j:j + 1, :]
        out_ref[r0:r0 + CONV_ROWS, lanes] = acc.astype(out_ref.dtype)


def _inproj_kernel(x_ref, g_ref, w_all_ref, b_glu_ref,
                   w_dw_ref, b_dw_ref, cv_ref, gate_ref, q0_ref, q1_ref, q2_ref,
                   h_ref, xn_ref, ext_ref, sh_ref):
    nx = D_MODEL // LANES
    tm = x_ref.shape[0]
    C = CONV_CHANNELS

    @pl.when(pl.program_id(0) == 0)
    def _():
        ext_ref[:, 0:HALO, :] = jnp.zeros((C // LANES, HALO, LANES), F32)

    xn = _rms(x_ref[...], g_ref[...])
    h_ref[0] = xn.astype(BF16)
    for k in range(nx):
        xn_ref[k] = xn[:, k * LANES:(k + 1) * LANES]
    for gi, (_, r) in enumerate(DILATED_GROUPS):
        if r == 1:
            continue
        n = tm // r
        for c in range(r):
            rows = jnp.concatenate([xn_ref[k, pl.ds(c, n, stride=r), :] for k in range(nx)], axis=-1)
            h_ref[gi, c * n:(c + 1) * n, :] = rows.astype(BF16)

    for k in range(C // LANES):
        val = slice(GLU_COL + k * LANES, GLU_COL + (k + 1) * LANES)
        gat = slice(GLU_COL + C + k * LANES, GLU_COL + C + (k + 1) * LANES)
        w_pair = jnp.concatenate([w_all_ref[:, val], w_all_ref[:, gat]], axis=1)
        z = _dot(h_ref[0], w_pair)
        zv = z[:, :LANES] + b_glu_ref[:, k * LANES:(k + 1) * LANES]
        zg = z[:, LANES:] + b_glu_ref[:, C + k * LANES:C + (k + 1) * LANES]
        ext_ref[k, HALO:HALO + tm, :] = zv * (jnp.tanh(zg) + 1.0)
        _causal_dwconv_chunk(k, ext_ref, sh_ref.at[k % 2], w_dw_ref, b_dw_ref[k], cv_ref)
    for k in range(C // LANES):
        ext_ref[k, 0:HALO, :] = ext_ref[k, tm:tm + HALO, :]

    for c0 in range(0, 2 * D_MODEL, GATE_PIECE):
        z = _dot(h_ref[0], w_all_ref[:, GATE_COL + c0:GATE_COL + c0 + GATE_PIECE])
        gate_ref[:, c0:c0 + GATE_PIECE] = _sigmoid_of_twice(z).astype(BF16)
    for gi, q_ref in enumerate((q0_ref, q1_ref, q2_ref)):
        for part in range(3):
            c0 = part * ATTN_W + gi * GROUP_W
            z = _dot(h_ref[gi], w_all_ref[:, c0:c0 + GROUP_W]).astype(BF16)
            q_ref[:, :, part * GROUP_W:(part + 1) * GROUP_W] = z.reshape(q_ref.shape[:2] + (GROUP_W,))


def _in_proj(x2, g, w_all, b_glu, w_dw, b_dw, tm=512):
    S = x2.shape[0]
    assert DILATED_GROUPS[0][1] == 1
    params = (g, w_all, b_glu, w_dw, b_dw)
    dil = [d for _, d in DILATED_GROUPS]
    return pl.pallas_call(
        _inproj_kernel,
        grid=(S // tm,),
        in_specs=[pl.BlockSpec((tm, D_MODEL), lambda i: (i, 0))] + [_resident(p) for p in params],
        out_specs=[pl.BlockSpec((tm, CONV_CHANNELS), lambda i: (i, 0)),
                   pl.BlockSpec((tm, 2 * D_MODEL), lambda i: (i, 0))]
                  + [pl.BlockSpec((r, tm // r, QKV_W), lambda i: (0, i, 0)) for r in dil],
        out_shape=[jax.ShapeDtypeStruct((S, CONV_CHANNELS), BF16),
                   jax.ShapeDtypeStruct((S, 2 * D_MODEL), BF16)]
                  + [jax.ShapeDtypeStruct((r, S // r, QKV_W), BF16) for r in dil],
        scratch_shapes=[pltpu.VMEM((N_GROUPS, tm, D_MODEL), BF16),
                        pltpu.VMEM((D_MODEL // LANES, tm, LANES), F32),
                        pltpu.VMEM((CONV_CHANNELS // LANES, HALO + tm, LANES), F32),
                        pltpu.VMEM((2, SUBLANES - 1, HALO + tm - SUBLANES, LANES), F32)],
        compiler_params=pltpu.CompilerParams(
            dimension_semantics=("arbitrary",), vmem_limit_bytes=VMEM_LIMIT),
        name="in_proj",
    )(x2, *params)


def _bucket_maps():
    max_exact = REL_BUCKETS // 2
    a = np.arange(BLOCK, dtype=np.int64)[:, None]
    c = np.arange(2 * BLOCK, dtype=np.int64)[None, :]
    offset = a - c + BLOCK
    maps = []
    for window, dilation in DILATED_GROUPS:
        span = window // dilation
        assert span == BLOCK
        d = np.maximum(offset * dilation, 0)
        df = np.maximum(d, 1).astype(np.float32)
        large = max_exact + (np.log(df / np.float32(max_exact)) / np.float32(math.log(REL_MAX_DISTANCE / max_exact))
                             * np.float32(REL_BUCKETS - max_exact)).astype(np.int32)
        large = np.minimum(large, REL_BUCKETS - 1)
        bucket = np.where(d < max_exact, d, large)
        valid = (offset >= 0) & (offset <= span)
        maps.append(np.where(valid, bucket, -1).astype(np.int32))
    return np.stack(maps)


def _bias_kernel(tab_ref, bucket_ref, o_ref):
    g = pl.program_id(0)
    bucket = bucket_ref[...]
    col = lax.broadcasted_iota(jnp.int32, bucket.shape, 1)
    in_bucket = [bucket == b for b in range(REL_BUCKETS)]
    for h in range(HEADS_PER_GROUP):
        acc = jnp.full(bucket.shape, NEG_INF, F32)
        for b in range(REL_BUCKETS):
            acc = jnp.where(in_bucket[b], tab_ref[b, g * HEADS_PER_GROUP + h] * LOG2E, acc)
        o_ref[0, h] = acc
        o_ref[1, h] = jnp.where(col < BLOCK, NEG_INF, acc)


def _bias_tables(rel_bias_table):
    return pl.pallas_call(
        _bias_kernel,
        grid=(N_GROUPS,),
        in_specs=[
            pl.BlockSpec(memory_space=pltpu.SMEM),
            pl.BlockSpec((None, BLOCK, 2 * BLOCK), lambda g: (g, 0, 0)),
        ],
        out_specs=pl.BlockSpec((None, 2, HEADS_PER_GROUP, BLOCK, 2 * BLOCK), lambda g: (g, 0, 0, 0, 0)),
        out_shape=jax.ShapeDtypeStruct((N_GROUPS, 2, HEADS_PER_GROUP, BLOCK, 2 * BLOCK), F32),
        name="bias_tables",
    )(rel_bias_table.astype(F32), jnp.asarray(_bucket_maps()))


def _attn_kernel(q_ref, kp_ref, kc_ref, vp_ref, vc_ref, bias_ref, sumcol_ref, o_ref, ml_ref):
    first = jnp.where(pl.program_id(1) == 0, 1, 0)
    lane = lax.broadcasted_iota(jnp.int32, (BLOCK, LANES), 1)
    low = lane < HEAD_DIM
    for j in range(q_ref.shape[0] // BLOCK):
        rows = slice(j * BLOCK, (j + 1) * BLOCK)
        variant = first if j == 0 else 0
        ml = jnp.zeros((BLOCK, LANES), F32)
        for pair in range(HEADS_PER_GROUP // 2):
            sl = slice(pair * LANES, (pair + 1) * LANES)
            q = q_ref[rows, sl]
            if j == 0:
                k = jnp.concatenate([kp_ref[:, sl], kc_ref[rows, sl]], axis=0)
                v = jnp.concatenate([vp_ref[:, sl], vc_ref[rows, sl]], axis=0)
            else:
                k = kc_ref[(j - 1) * BLOCK:(j + 1) * BLOCK, sl]
                v = vc_ref[(j - 1) * BLOCK:(j + 1) * BLOCK, sl]
            halves = []
            for sub in range(2):
                h = 2 * pair + sub
                qm = jnp.where(low if sub == 0 else ~low, q, jnp.zeros_like(q))
                s = lax.dot_general(qm, k, (((1,), (1,)), ((), ())), preferred_element_type=F32)
                t = s + bias_ref[variant, h]
                mt = jnp.max(t, axis=-1, keepdims=True)
                p = jnp.exp2(t - mt).astype(BF16)
                r = _dot(p, jnp.concatenate([v, sumcol_ref[h]], axis=1))
                halves.append(r[:, :LANES])
                ml = jnp.where(lane == h, mt, ml) + r[:, LANES:]
            o_ref[rows, sl] = jnp.where(low, halves[0], halves[1]).astype(o_ref.dtype)
        ml_ref[rows, :] = ml


def _attention_group(qkv, biases, g, nq=8):
    r, L, _ = qkv.shape
    tq = nq * BLOCK
    sumcol = jnp.asarray(np.broadcast_to(
        np.arange(LANES)[None, None, :] == HEADS_PER_GROUP + np.arange(HEADS_PER_GROUP)[:, None, None],
        (HEADS_PER_GROUP, 2 * BLOCK, LANES)), dtype=BF16)

    def spec(part, prev):
        if prev:
            return pl.BlockSpec((None, BLOCK, GROUP_W), lambda c, i: (c, jnp.maximum(i * nq - 1, 0), part))
        return pl.BlockSpec((None, tq, GROUP_W), lambda c, i: (c, i, part))

    return pl.pallas_call(
        _attn_kernel,
        grid=(r, L // tq),
        in_specs=[
            spec(0, False),
            spec(1, True), spec(1, False),
            spec(2, True), spec(2, False),
            pl.BlockSpec((None, 2, HEADS_PER_GROUP, BLOCK, 2 * BLOCK), lambda c, i: (g, 0, 0, 0, 0)),
            pl.BlockSpec(sumcol.shape, lambda c, i: (0, 0, 0)),
        ],
        out_specs=[
            pl.BlockSpec((None, tq, GROUP_W), lambda c, i: (c, i, 0)),
            pl.BlockSpec((None, tq, LSE_W), lambda c, i: (c, i, 0)),
        ],
        out_shape=[
            jax.ShapeDtypeStruct((r, L, GROUP_W), BF16),
            jax.ShapeDtypeStruct((r, L, LSE_W), F32),
        ],
        compiler_params=pltpu.CompilerParams(
            dimension_semantics=("parallel", "arbitrary"), vmem_limit_bytes=VMEM_LIMIT),
        name=f"attn_g{g}",
    )(qkv, qkv, qkv, qkv, qkv, biases, sumcol)


def _mixer_kernel(o0_ref, o1_ref, o2_ref, l0_ref, l1_ref, l2_ref, cv_ref, ga_ref, gc_ref, x_ref,
                  expand_ref, g_ln_ref, b_ln_ref, w_co_ref, b_co_ref, w_ao_ref, w_mo_ref, g_pm_ref,
                  out_ref, on_ref, ln_ref):
    def natural(src_ref, dst_ref):
        r, n, w = src_ref.shape
        if r == 1:
            return src_ref[0].astype(F32)
        for c in range(r):
            for k in range(w // LANES):
                dst_ref[k, pl.ds(c, n, stride=r), :] = src_ref[c, :, k * LANES:(k + 1) * LANES].astype(F32)
        return jnp.concatenate([dst_ref[k] for k in range(w // LANES)], axis=-1)

    ml = [natural(r_, ln_ref.at[g]) for g, r_ in enumerate((l0_ref, l1_ref, l2_ref))]
    den = [pltpu.roll(a, LANES - HEADS_PER_GROUP, axis=1) for a in ml]
    mx = jnp.maximum(jnp.maximum(ml[0], ml[1]), ml[2])
    w = [jnp.exp2(a - mx) for a in ml]
    total = w[0] * den[0] + w[1] * den[1] + w[2] * den[2]
    head_lane = lax.broadcasted_iota(jnp.int32, total.shape, 1) < HEADS_PER_GROUP
    inv = jnp.where(head_lane, 1.0 / total, 0.0)
    expand = expand_ref[...]

    def widen(a):
        hi = a.astype(BF16)
        lo = (a - hi.astype(F32)).astype(BF16)
        return _dot(jnp.concatenate([hi, lo], axis=1), expand)

    o = (widen(w[0] * inv) * natural(o0_ref, on_ref.at[0])
         + widen(w[1] * inv) * natural(o1_ref, on_ref.at[1])
         + widen(w[2] * inv) * natural(o2_ref, on_ref.at[2]))
    y_attn = _dot(o.astype(BF16), w_ao_ref[...])

    acc = cv_ref[...].astype(F32)
    mu = jnp.mean(acc, axis=-1, keepdims=True)
    xc = acc - mu
    y = xc * lax.rsqrt(jnp.mean(xc * xc, axis=-1, keepdims=True) + LN_EPS)
    y = _silu_of_twice(y * g_ln_ref[...] + b_ln_ref[...])
    y_conv = _dot(y.astype(BF16), w_co_ref[...]) + b_co_ref[...]

    merged = ga_ref[...].astype(F32) * y_attn + gc_ref[...].astype(F32) * y_conv
    mix = _dot(merged.astype(BF16), w_mo_ref[...])
    out_ref[...] = x_ref[...] + _rms(mix, g_pm_ref[...])


def _mixer(x2, cv, gates, os_, stats, params, tm=512):
    S = x2.shape[0]

    def row(w, col=0):
        return pl.BlockSpec((tm, w), lambda i: (i, col))

    def rspec(a):
        r, _, w = a.shape
        return pl.BlockSpec((r, tm // r, w), lambda i: (0, i, 0))

    return pl.pallas_call(
        _mixer_kernel,
        grid=(S // tm,),
        in_specs=[rspec(a) for a in os_] + [rspec(a) for a in stats]
                 + [row(CONV_CHANNELS), row(D_MODEL, 0), row(D_MODEL, 1), row(D_MODEL)]
                 + [_resident(p) for p in params],
        out_specs=row(D_MODEL),
        out_shape=jax.ShapeDtypeStruct((S, D_MODEL), F32),
        scratch_shapes=[pltpu.VMEM((N_GROUPS, GROUP_W // LANES, tm, LANES), F32),
                        pltpu.VMEM((N_GROUPS, LSE_W // LANES, tm, LANES), F32)],
        compiler_params=pltpu.CompilerParams(
            dimension_semantics=("parallel",), vmem_limit_bytes=VMEM_LIMIT),
        name="mixer",
    )(*os_, *stats, cv, gates, gates, x2, *params)


def _ffn_kernel(x_ref, g_pre_ref, w_in_ref, w_out_ref, g_post_ref, out_ref, h_ref, act_ref, *, chunk):
    x = x_ref[...]
    h_ref[...] = _rms(x, g_pre_ref[...]).astype(BF16)
    for c0 in range(0, FFN_HIDDEN, chunk):
        half_gate = _dot(h_ref[...], w_in_ref[:, c0:c0 + chunk])
        up = _dot(h_ref[...], w_in_ref[:, FFN_HIDDEN + c0:FFN_HIDDEN + c0 + chunk])
        act_ref[:, c0:c0 + chunk] = (_silu_of_twice(half_gate) * up).astype(BF16)
    out_ref[...] = x + _rms(_dot(act_ref[...], w_out_ref[...]), g_post_ref[...])


def _ffn(x1, g_pre, w_in, w_out, g_post, tm=512, chunk=256):
    S = x1.shape[0]
    row = pl.BlockSpec((tm, D_MODEL), lambda i: (i, 0))
    params = (g_pre, w_in, w_out, g_post)
    return pl.pallas_call(
        functools.partial(_ffn_kernel, chunk=chunk),
        grid=(S // tm,),
        in_specs=[row] + [_resident(p) for p in params],
        out_specs=row,
        out_shape=jax.ShapeDtypeStruct((S, D_MODEL), F32),
        scratch_shapes=[pltpu.VMEM((tm, D_MODEL), BF16), pltpu.VMEM((tm, FFN_HIDDEN), BF16)],
        compiler_params=pltpu.CompilerParams(
            dimension_semantics=("parallel",), vmem_limit_bytes=VMEM_LIMIT),
        name="ffn",
    )(x1, *params)


def _lane_chunks(a):
    return a.reshape(a.shape[0], -1, LANES).transpose(1, 0, 2)


def _scaled_cols(w, *ranges):
    scale = np.ones(w.shape[1], np.float32)
    for lo, hi, s in ranges:
        scale[lo:hi] = s
    return (w * jnp.asarray(scale)).astype(BF16)


def kernel(x, rel_bias_table, g_pre_mix, w_in, b_glu, w_dw, b_dw, g_conv_ln, b_conv_ln, w_conv_out, b_conv_out, w_attn_out, w_mix_out, g_post_mix, g_pre_ffn, w_ffn_in, w_ffn_out, g_post_ffn):
    B, S, D = x.shape
    depth = w_in.shape[0]
    expand = jnp.asarray(
        (np.arange(2 * LSE_W)[:, None] % LSE_W == np.arange(GROUP_W)[None, :] // HEAD_DIM), dtype=BF16)
    biases = _bias_tables(rel_bias_table)
    outs = []
    for b in range(B):
        xb = x[b]
        for l in range(depth):
            r2 = lambda a: a[l].reshape(1, -1)
            w_all = _scaled_cols(w_in[l], (0, ATTN_W, QK_SCALE), (GLU_COL, w_in.shape[2], 0.5))
            cv, gates, *qkvs = _in_proj(xb, r2(g_pre_mix), w_all, 0.5 * r2(b_glu),
                                        _lane_chunks(w_dw[l]), _lane_chunks(r2(b_dw)))
            os_, stats = [], []
            for g, qkv in enumerate(qkvs):
                o_g, ml_g = _attention_group(qkv, biases, g)
                os_.append(o_g)
                stats.append(ml_g)
            params = (expand, 0.5 * r2(g_conv_ln), 0.5 * r2(b_conv_ln), w_conv_out[l].astype(BF16),
                      r2(b_conv_out), w_attn_out[l].astype(BF16), w_mix_out[l].astype(BF16), r2(g_post_mix))
            x1 = _mixer(xb, cv, gates, os_, stats, params)
            xb = _ffn(x1, r2(g_pre_ffn), _scaled_cols(w_ffn_in[l], (0, FFN_HIDDEN, 0.5)),
                      w_ffn_out[l].astype(BF16), r2(g_post_ffn))
        outs.append(xb)
    return jnp.stack(outs, axis=0)
```

```python
import functools
import math

import jax
import jax.numpy as jnp
import numpy as np
from jax import lax
from jax.experimental import pallas as pl
from jax.experimental.pallas import tpu as pltpu

D_MODEL = 1024
HEAD_DIM = 64
HEADS_PER_GROUP = 8
DILATED_GROUPS = ((128, 1), (512, 4), (2048, 16))
N_GROUPS = len(DILATED_GROUPS)
GROUP_W = HEADS_PER_GROUP * HEAD_DIM
ATTN_W = N_GROUPS * GROUP_W
QKV_W = 3 * GROUP_W
BLOCK = 128
REL_BUCKETS = 32
REL_MAX_DISTANCE = 2048
CONV_CHANNELS = D_MODEL
CONV_WIDTH = 31
FFN_HIDDEN = 2816
RMS_EPS = 1e-6
LN_EPS = 1e-5
NEG_INF = -1e30
LOG2E = math.log2(math.e)
QK_SCALE = HEAD_DIM ** -0.5 * LOG2E
GLU_COL = 3 * ATTN_W
GATE_COL = GLU_COL + 2 * CONV_CHANNELS

LANES = 128
SUBLANES = 8
CONV_ROWS = 64
LSE_W = LANES
HALO = 32
GATE_PIECE = 256
VMEM_LIMIT = 56 * 1024 * 1024

BF16 = jnp.bfloat16
F32 = jnp.float32


def _sigmoid_of_twice(h):
    return 0.5 * jnp.tanh(h) + 0.5


def _silu_of_twice(h):
    return h * (jnp.tanh(h) + 1.0)


def _rms(v, g):
    return v * lax.rsqrt(jnp.mean(v * v, axis=-1, keepdims=True) + RMS_EPS) * g


def _dot(a, b):
    return jnp.dot(a, b, preferred_element_type=F32)


def _resident(a):
    return pl.BlockSpec(a.shape, lambda *_: (0,) * a.ndim, pipeline_mode=pl.Buffered(1))


def _causal_dwconv_chunk(k, ext_ref, sh_ref, w_ref, bias, out_ref):
    tm = out_ref.shape[0]
    ns = sh_ref.shape[1]
    ext = ext_ref.at[k]
    lanes = slice(k * LANES, (k + 1) * LANES)
    for s in range(1, SUBLANES):
        sh_ref[s - 1] = ext[s:s + ns, :]
    for r0 in range(0, tm, CONV_ROWS):
        acc = jnp.zeros((CONV_ROWS, LANES), F32) + bias
        for j in range(CONV_WIDTH):
            a, s = divmod(HALO - (CONV_WIDTH - 1) + j, SUBLANES)
            src = ext if s == 0 else sh_ref.at[s - 1]
            row = a * SUBLANES + r0
            acc = acc + src[row:row + CONV_ROWS, :] * w_ref[k, j:j + 1, :]
        out_ref[r0:r0 + CONV_ROWS, lanes] = acc.astype(out_ref.dtype)


def _inproj_kernel(x_ref, g_ref, w_all_ref, b_glu_ref,
                   w_dw_ref, b_dw_ref, cv_ref, gate_ref, q0_ref, q1_ref, q2_ref,
                   h_ref, xn_ref, ext_ref, sh_ref):
    nx = D_MODEL // LANES
    tm = x_ref.shape[0]
    C = CONV_CHANNELS

    @pl.when(pl.program_id(0) == 0)
    def _():
        ext_ref[:, 0:HALO, :] = jnp.zeros((C // LANES, HALO, LANES), F32)

    xn = _rms(x_ref[...], g_ref[...])
    h_ref[0] = xn.astype(BF16)
    for k in range(nx):
        xn_ref[k] = xn[:, k * LANES:(k + 1) * LANES]
    for gi, (_, r) in enumerate(DILATED_GROUPS):
        if r == 1:
            continue
        n = tm // r
        for c in range(r):
            rows = jnp.concatenate([xn_ref[k, pl.ds(c, n, stride=r), :] for k in range(nx)], axis=-1)
            h_ref[gi, c * n:(c + 1) * n, :] = rows.astype(BF16)

    for k in range(C // LANES):
        val = slice(GLU_COL + k * LANES, GLU_COL + (k + 1) * LANES)
        gat = slice(GLU_COL + C + k * LANES, GLU_COL + C + (k + 1) * LANES)
        w_pair = jnp.concatenate([w_all_ref[:, val], w_all_ref[:, gat]], axis=1)
        z = _dot(h_ref[0], w_pair)
        zv = z[:, :LANES] + b_glu_ref[:, k * LANES:(k + 1) * LANES]
        zg = z[:, LANES:] + b_glu_ref[:, C + k * LANES:C + (k + 1) * LANES]
        ext_ref[k, HALO:HALO + tm, :] = zv * (jnp.tanh(zg) + 1.0)
        _causal_dwconv_chunk(k, ext_ref, sh_ref.at[k % 2], w_dw_ref, b_dw_ref[k], cv_ref)
    for k in range(C // LANES):
        ext_ref[k, 0:HALO, :] = ext_ref[k, tm:tm + HALO, :]

    for c0 in range(0, 2 * D_MODEL, GATE_PIECE):
        z = _dot(h_ref[0], w_all_ref[:, GATE_COL + c0:GATE_COL + c0 + GATE_PIECE])
        gate_ref[:, c0:c0 + GATE_PIECE] = _sigmoid_of_twice(z).astype(BF16)
    for gi, q_ref in enumerate((q0_ref, q1_ref, q2_ref)):
        for part in range(3):
            c0 = part * ATTN_W + gi * GROUP_W
            z = _dot(h_ref[gi], w_all_ref[:, c0:c0 + GROUP_W]).astype(BF16)
            q_ref[:, :, part * GROUP_W:(part + 1) * GROUP_W] = z.reshape(q_ref.shape[:2] + (GROUP_W,))


def _in_proj(x2, g, w_all, b_glu, w_dw, b_dw, tm=512):
    S = x2.shape[0]
    assert DILATED_GROUPS[0][1] == 1
    params = (g, w_all, b_glu, w_dw, b_dw)
    dil = [d for _, d in DILATED_GROUPS]
    return pl.pallas_call(
        _inproj_kernel,
        grid=(S // tm,),
        in_specs=[pl.BlockSpec((tm, D_MODEL), lambda i: (i, 0))] + [_resident(p) for p in params],
        out_specs=[pl.BlockSpec((tm, CONV_CHANNELS), lambda i: (i, 0)),
                   pl.BlockSpec((tm, 2 * D_MODEL), lambda i: (i, 0))]
                  + [pl.BlockSpec((r, tm // r, QKV_W), lambda i: (0, i, 0)) for r in dil],
        out_shape=[jax.ShapeDtypeStruct((S, CONV_CHANNELS), BF16),
                   jax.ShapeDtypeStruct((S, 2 * D_MODEL), BF16)]
                  + [jax.ShapeDtypeStruct((r, S // r, QKV_W), BF16) for r in dil],
        scratch_shapes=[pltpu.VMEM((N_GROUPS, tm, D_MODEL), BF16),
                        pltpu.VMEM((D_MODEL // LANES, tm, LANES), F32),
                        pltpu.VMEM((CONV_CHANNELS // LANES, HALO + tm, LANES), F32),
                        pltpu.VMEM((2, SUBLANES - 1, HALO + tm - SUBLANES, LANES), F32)],
        compiler_params=pltpu.CompilerParams(
            dimension_semantics=("arbitrary",), vmem_limit_bytes=VMEM_LIMIT),
        name="in_proj",
    )(x2, *params)


def _bucket_maps():
    max_exact = REL_BUCKETS // 2
    a = np.arange(BLOCK, dtype=np.int64)[:, None]
    c = np.arange(2 * BLOCK, dtype=np.int64)[None, :]
    offset = a - c + BLOCK
    maps = []
    for window, dilation in DILATED_GROUPS:
        span = window // dilation
        assert span == BLOCK
        d = np.maximum(offset * dilation, 0)
        df = np.maximum(d, 1).astype(np.float32)
        large = max_exact + (np.log(df / np.float32(max_exact)) / np.float32(math.log(REL_MAX_DISTANCE / max_exact))
                             * np.float32(REL_BUCKETS - max_exact)).astype(np.int32)
        large = np.minimum(large, REL_BUCKETS - 1)
        bucket = np.where(d < max_exact, d, large)
        valid = (offset >= 0) & (offset <= span)
        maps.append(np.where(valid, bucket, -1).astype(np.int32))
    return np.stack(maps)


def _bias_kernel(tab_ref, bucket_ref, o_ref):
    g = pl.program_id(0)
    bucket = bucket_ref[...]
    col = lax.broadcasted_iota(jnp.int32, bucket.shape, 1)
    in_bucket = [bucket == b for b in range(REL_BUCKETS)]
    for h in range(HEADS_PER_GROUP):
        acc = jnp.full(bucket.shape, NEG_INF, F32)
        for b in range(REL_BUCKETS):
            acc = jnp.where(in_bucket[b], tab_ref[b, g * HEADS_PER_GROUP + h] * LOG2E, acc)
        o_ref[0, h] = acc
        o_ref[1, h] = jnp.where(col < BLOCK, NEG_INF, acc)


def _bias_tables(rel_bias_table):
    return pl.pallas_call(
        _bias_kernel,
        grid=(N_GROUPS,),
        in_specs=[
            pl.BlockSpec(memory_space=pltpu.SMEM),
            pl.BlockSpec((None, BLOCK, 2 * BLOCK), lambda g: (g, 0, 0)),
        ],
        out_specs=pl.BlockSpec((None, 2, HEADS_PER_GROUP, BLOCK, 2 * BLOCK), lambda g: (g, 0, 0, 0, 0)),
        out_shape=jax.ShapeDtypeStruct((N_GROUPS, 2, HEADS_PER_GROUP, BLOCK, 2 * BLOCK), F32),
        name="bias_tables",
    )(rel_bias_table.astype(F32), jnp.asarray(_bucket_maps()))


def _attn_kernel(q_ref, kp_ref, kc_ref, vp_ref, vc_ref, bias_ref, sumcol_ref, o_ref, ml_ref):
    first = jnp.where(pl.program_id(1) == 0, 1, 0)
    lane = lax.broadcasted_iota(jnp.int32, (BLOCK, LANES), 1)
    low = lane < HEAD_DIM
    for j in range(q_ref.shape[0] // BLOCK):
        rows = slice(j * BLOCK, (j + 1) * BLOCK)
        variant = first if j == 0 else 0
        ml = jnp.zeros((BLOCK, LANES), F32)
        for pair in range(HEADS_PER_GROUP // 2):
            sl = slice(pair * LANES, (pair + 1) * LANES)
            q = q_ref[rows, sl]
            if j == 0:
                k = jnp.concatenate([kp_ref[:, sl], kc_ref[rows, sl]], axis=0)
                v = jnp.concatenate([vp_ref[:, sl], vc_ref[rows, sl]], axis=0)
            else:
                k = kc_ref[(j - 1) * BLOCK:(j + 1) * BLOCK, sl]
                v = vc_ref[(j - 1) * BLOCK:(j + 1) * BLOCK, sl]
            halves = []
            for sub in range(2):
                h = 2 * pair + sub
                qm = jnp.where(low if sub == 0 else ~low, q, jnp.zeros_like(q))
                s = lax.dot_general(qm, k, (((1,), (1,)), ((), ())), preferred_element_type=F32)
                t = s + bias_ref[variant, h]
                mt = jnp.max(t, axis=-1, keepdims=True)
                p = jnp.exp2(t - mt).astype(BF16)
                r = _dot(p, jnp.concatenate([v, sumcol_ref[h]], axis=1))
                halves.append(r[:, :LANES])
                ml = jnp.where(lane == h, mt, ml) + r[:, LANES:]
            o_ref[rows, sl] = jnp.where(low, halves[0], halves[1]).astype(o_ref.dtype)
        ml_ref[rows, :] = ml


def _attention_group(qkv, biases, g, nq=8):
    r, L, _ = qkv.shape
    tq = nq * BLOCK
    sumcol = jnp.asarray(np.broadcast_to(
        np.arange(LANES)[None, None, :] == HEADS_PER_GROUP + np.arange(HEADS_PER_GROUP)[:, None, None],
        (HEADS_PER_GROUP, 2 * BLOCK, LANES)), dtype=BF16)

    def spec(part, prev):
        if prev:
            return pl.BlockSpec((None, BLOCK, GROUP_W), lambda c, i: (c, jnp.maximum(i * nq - 1, 0), part))
        return pl.BlockSpec((None, tq, GROUP_W), lambda c, i: (c, i, part))

    return pl.pallas_call(
        _attn_kernel,
        grid=(r, L // tq),
        in_specs=[
            spec(0, False),
            spec(1, True), spec(1, False),
            spec(2, True), spec(2, False),
            pl.BlockSpec((None, 2, HEADS_PER_GROUP, BLOCK, 2 * BLOCK), lambda c, i: (g, 0, 0, 0, 0)),
            pl.BlockSpec(sumcol.shape, lambda c, i: (0, 0, 0)),
        ],
        out_specs=[
            pl.BlockSpec((None, tq, GROUP_W), lambda c, i: (c, i, 0)),
            pl.BlockSpec((None, tq, LSE_W), lambda c, i: (c, i, 0)),
        ],
        out_shape=[
            jax.ShapeDtypeStruct((r, L, GROUP_W), BF16),
            jax.ShapeDtypeStruct((r, L, LSE_W), F32),
        ],
        compiler_params=pltpu.CompilerParams(
            dimension_semantics=("parallel", "arbitrary"), vmem_limit_bytes=VMEM_LIMIT),
        name=f"attn_g{g}",
    )(qkv, qkv, qkv, qkv, qkv, biases, sumcol)


def _mixer_kernel(o0_ref, o1_ref, o2_ref, l0_ref, l1_ref, l2_ref, cv_ref, ga_ref, gc_ref, x_ref,
                  expand_ref, g_ln_ref, b_ln_ref, w_co_ref, b_co_ref, w_ao_ref, w_mo_ref, g_pm_ref,
                  out_ref, on_ref, ln_ref):
    def natural(src_ref, dst_ref):
        r, n, w = src_ref.shape
        if r == 1:
            return src_ref[0].astype(F32)
        for c in range(r):
            for k in range(w // LANES):
                dst_ref[k, pl.ds(c, n, stride=r), :] = src_ref[c, :, k * LANES:(k + 1) * LANES].astype(F32)
        return jnp.concatenate([dst_ref[k] for k in range(w // LANES)], axis=-1)

    ml = [natural(r_, ln_ref.at[g]) for g, r_ in enumerate((l0_ref, l1_ref, l2_ref))]
    den = [pltpu.roll(a, LANES - HEADS_PER_GROUP, axis=1) for a in ml]
    mx = jnp.maximum(jnp.maximum(ml[0], ml[1]), ml[2])
    w = [jnp.exp2(a - mx) for a in ml]
    total = w[0] * den[0] + w[1] * den[1] + w[2] * den[2]
    head_lane = lax.broadcasted_iota(jnp.int32, total.shape, 1) < HEADS_PER_GROUP
    inv = jnp.where(head_lane, 1.0 / total, 0.0)
    expand = expand_ref[...]

    def widen(a):
        hi = a.astype(BF16)
        lo = (a - hi.astype(F32)).astype(BF16)
        return _dot(jnp.concatenate([hi, lo], axis=1), expand)

    o = (widen(w[0] * inv) * natural(o0_ref, on_ref.at[0])
         + widen(w[1] * inv) * natural(o1_ref, on_ref.at[1])
         + widen(w[2] * inv) * natural(o2_ref, on_ref.at[2]))
    y_attn = _dot(o.astype(BF16), w_ao_ref[...])

    acc = cv_ref[...].astype(F32)
    mu = jnp.mean(acc, axis=-1, keepdims=True)
    xc = acc - mu
    y = xc * lax.rsqrt(jnp.mean(xc * xc, axis=-1, keepdims=True) + LN_EPS)
    y = _silu_of_twice(y * g_ln_ref[...] + b_ln_ref[...])
    y_conv = _dot(y.astype(BF16), w_co_ref[...]) + b_co_ref[...]

    merged = ga_ref[...].astype(F32) * y_attn + gc_ref[...].astype(F32) * y_conv
    mix = _dot(merged.astype(BF16), w_mo_ref[...])
    out_ref[...] = x_ref[...] + _rms(mix, g_pm_ref[...])


def _mixer(x2, cv, gates, os_, stats, params, tm=512):
    S = x2.shape[0]

    def row(w, col=0):
        return pl.BlockSpec((tm, w), lambda i: (i, col))

    def rspec(a):
        r, _, w = a.shape
        return pl.BlockSpec((r, tm // r, w), lambda i: (0, i, 0))

    return pl.pallas_call(
        _mixer_kernel,
        grid=(S // tm,),
        in_specs=[rspec(a) for a in os_] + [rspec(a) for a in stats]
                 + [row(CONV_CHANNELS), row(D_MODEL, 0), row(D_MODEL, 1), row(D_MODEL)]
                 + [_resident(p) for p in params],
        out_specs=row(D_MODEL),
        out_shape=jax.ShapeDtypeStruct((S, D_MODEL), F32),
        scratch_shapes=[pltpu.VMEM((N_GROUPS, GROUP_W // LANES, tm, LANES), F32),
                        pltpu.VMEM((N_GROUPS, LSE_W // LANES, tm, LANES), F32)],
        compiler_params=pltpu.CompilerParams(
            dimension_semantics=("parallel",), vmem_limit_bytes=VMEM_LIMIT),
        name="mixer",
    )(*os_, *stats, cv, gates, gates, x2, *params)


def _ffn_kernel(x_ref, g_pre_ref, w_in_ref, w_out_ref, g_post_ref, out_ref, h_ref, act_ref, *, chunk):
    x = x_ref[...]
    h_ref[...] = _rms(x, g_pre_ref[...]).astype(BF16)
    for c0 in range(0, FFN_HIDDEN, chunk):
        half_gate = _dot(h_ref[...], w_in_ref[:, c0:c0 + chunk])
        up = _dot(h_ref[...], w_in_ref[:, FFN_HIDDEN + c0:FFN_HIDDEN + c0 + chunk])
        act_ref[:, c0:c0 + chunk] = (_silu_of_twice(half_gate) * up).astype(BF16)
    out_ref[...] = x + _rms(_dot(act_ref[...], w_out_ref[...]), g_post_ref[...])


def _ffn(x1, g_pre, w_in, w_out, g_post, tm=1024, chunk=256):
    S = x1.shape[0]
    row = pl.BlockSpec((tm, D_MODEL), lambda i: (i, 0))
    params = (g_pre, w_in, w_out, g_post)
    return pl.pallas_call(
        functools.partial(_ffn_kernel, chunk=chunk),
        grid=(S // tm,),
        in_specs=[row] + [_resident(p) for p in params],
        out_specs=row,
        out_shape=jax.ShapeDtypeStruct((S, D_MODEL), F32),
        scratch_shapes=[pltpu.VMEM((tm, D_MODEL), BF16), pltpu.VMEM((tm, FFN_HIDDEN), BF16)],
        compiler_params=pltpu.CompilerParams(
            dimension_semantics=("parallel",), vmem_limit_bytes=VMEM_LIMIT),
        name="ffn",
    )(x1, *params)


def _lane_chunks(a):
    return a.reshape(a.shape[0], -1, LANES).transpose(1, 0, 2)


def _scaled_cols(w, *ranges):
    scale = np.ones(w.shape[1], np.float32)
    for lo, hi, s in ranges:
        scale[lo:hi] = s
    return (w * jnp.asarray(scale)).astype(BF16)


def kernel(x, rel_bias_table, g_pre_mix, w_in, b_glu, w_dw, b_dw, g_conv_ln, b_conv_ln, w_conv_out, b_conv_out, w_attn_out, w_mix_out, g_post_mix, g_pre_ffn, w_ffn_in, w_ffn_out, g_post_ffn):
    B, S, D = x.shape
    depth = w_in.shape[0]
    expand = jnp.asarray(
        (np.arange(2 * LSE_W)[:, None] % LSE_W == np.arange(GROUP_W)[None, :] // HEAD_DIM), dtype=BF16)
    biases = _bias_tables(rel_bias_table)
    outs = []
    for b in range(B):
        xb = x[b]
        for l in range(depth):
            r2 = lambda a: a[l].reshape(1, -1)
            w_all = _scaled_cols(w_in[l], (0, ATTN_W, QK_SCALE), (GLU_COL, w_in.shape[2], 0.5))
            cv, gates, *qkvs = _in_proj(xb, r2(g_pre_mix), w_all, 0.5 * r2(b_glu),
                                        _lane_chunks(w_dw[l]), _lane_chunks(r2(b_dw)))
            os_, stats = [], []
            for g, qkv in enumerate(qkvs):
                o_g, ml_g = _attention_group(qkv, biases, g)
                os_.append(o_g)
                stats.append(ml_g)
            params = (expand, 0.5 * r2(g_conv_ln), 0.5 * r2(b_conv_ln), w_conv_out[l].astype(BF16),
                      r2(b_conv_out), w_attn_out[l].astype(BF16), w_mix_out[l].astype(BF16), r2(g_post_mix))
            x1 = _mixer(xb, cv, gates, os_, stats, params)
            xb = _ffn(x1, r2(g_pre_ffn), _scaled_cols(w_ffn_in[l], (0, FFN_HIDDEN, 0.5)),
                      w_ffn_out[l].astype(BF16), r2(g_post_ffn))
        outs.append(xb)
    return jnp.stack(outs, axis=0)
```
